```python
import math
import jax, jax.numpy as jnp
from jax import lax
import numpy as np

D_MODEL = 4096
BATCH = 4
SEQ = 2048
DEPTH = 4
DEC_BATCH = 8
DEC_SEQ = 1
PAST_LEN = 8192
PAGE_SIZE = 128

WIDTH_A = D_MODEL // 2
DK_A = 64
DV_A = 2 * DK_A
H_A = WIDTH_A // DV_A
Q_BLOCK = 128
NUM_BUCKETS = 32
MAX_DISTANCE = 128
D_INNER_B = D_MODEL // 2
HEADDIM_B = 64
H_B = D_INNER_B // HEADDIM_B
G_B = 4
N_B = 128
CONV_DIM_B = D_INNER_B + 2 * G_B * N_B
CHUNK_B = 64
DK_C = 128
DV_C = 128
H_C = (D_MODEL // 2) // DV_C
WIDTH_C = H_C * DV_C
CONV_DIM_C = 2 * H_C * DK_C + WIDTH_C
CHUNK_C = 64
CONV_K = 4
N_BRANCH = 3
EPS = 1e-6
SPLIT_SIZES = (
    H_A * 2 * DK_A,
    H_A * 2 * DK_A,
    WIDTH_A,
    WIDTH_A,
    D_INNER_B,
    CONV_DIM_B,
    H_B,
    CONV_DIM_C,
    WIDTH_C,
    H_C,
    H_C,
    N_BRANCH * D_MODEL,
)
W_IN = sum(SPLIT_SIZES)

kernel_name = 'hybrid_diffattn_ssd_gdn_step'


def rmsnorm(x, w):
    xf = x.astype(jnp.float32)
    xf = xf * lax.rsqrt(jnp.mean(xf * xf, axis=-1, keepdims=True) + EPS)
    return xf.astype(x.dtype) * w


def l2norm(x):
    xf = x.astype(jnp.float32)
    return (xf * lax.rsqrt(jnp.sum(xf * xf, axis=-1, keepdims=True) + EPS)).astype(x.dtype)


def pad_seq(t, pad):
    return jnp.pad(t, [(0, 0), (0, pad)] + [(0, 0)] * (t.ndim - 2))


def causal_conv(x, w, bias, prev):
    L = x.shape[1]
    xp = jnp.concatenate([prev.astype(x.dtype), x], axis=1)
    y = xp[:, 0:L] * w[0]
    for i in range(1, CONV_K):
        y = y + xp[:, i:i + L] * w[i]
    if bias is not None:
        y = y + bias
    return y, xp[:, xp.shape[1] - (CONV_K - 1):]


def rel_bias_logits(q_pos, k_pos, table):
    n = jnp.maximum(q_pos[:, None] - k_pos[None, :], 0)
    max_exact = NUM_BUCKETS // 2
    nf = jnp.maximum(n, 1).astype(jnp.float32)
    large = max_exact + (jnp.log(nf / max_exact) / math.log(MAX_DISTANCE / max_exact)
                         * (NUM_BUCKETS - max_exact)).astype(jnp.int32)
    bucket = jnp.where(n < max_exact, n, jnp.minimum(large, NUM_BUCKETS - 1))
    return jnp.transpose(table[bucket], (2, 0, 1)).astype(jnp.float32)


def diff_attend(q, k, v, bias, mask, lam):
    s = jnp.einsum('bqhmd,bkhmd->bmhqk', q, k).astype(jnp.float32) * (DK_A ** -0.5) + bias
    s = jnp.where(mask, s, -1e30)
    p = jax.nn.softmax(s, axis=-1)
    a = p[:, 0] - lam.astype(jnp.float32) * p[:, 1]
    return jnp.einsum('bhqk,bkhd->bqhd', a.astype(v.dtype), v)


def diff_attn_prompt(q, k, v, table, lam):
    b, L = q.shape[:2]
    nb = L // Q_BLOCK
    qb = jnp.moveaxis(q.reshape(b, nb, Q_BLOCK, H_A, 2, DK_A), 1, 0)
    k_pos = jnp.arange(L)

    def one_block(args):
        q_i, i = args
        q_pos = i * Q_BLOCK + jnp.arange(Q_BLOCK)
        bias = rel_bias_logits(q_pos, k_pos, table)
        mask = k_pos[None, :] <= q_pos[:, None]
        return diff_attend(q_i, k, v, bias, mask, lam)

    o = lax.map(one_block, (qb, jnp.arange(nb)))
    return jnp.moveaxis(o, 0, 1).reshape(b, L, H_A, DV_A)


def diff_attn_sample(q, k_new, v_new, past_k, past_v, table, lam):
    past = past_k.shape[1]
    Ls = q.shape[1]
    k = jnp.concatenate([past_k.astype(k_new.dtype), k_new], axis=1)
    v = jnp.concatenate([past_v.astype(v_new.dtype), v_new], axis=1)
    q_pos = past + jnp.arange(Ls)
    k_pos = jnp.arange(past + Ls)
    bias = rel_bias_logits(q_pos, k_pos, table)
    mask = k_pos[None, :] <= q_pos[:, None]
    return diff_attend(q, k, v, bias, mask, lam)


def ssd_scan(x, dt, A, Bm, Cm, h0):
    b, L = x.shape[:2]
    R = H_B // G_B
    Q = min(CHUNK_B, L)
    pad = (-L) % Q
    nc = (L + pad) // Q
    x, dt, Bm, Cm = pad_seq(x, pad), pad_seq(dt, pad), pad_seq(Bm, pad), pad_seq(Cm, pad)
    xdt = (x * dt[..., None]).reshape(b, nc, Q, G_B, R, HEADDIM_B)
    a = (dt * A).astype(jnp.float32).reshape(b, nc, Q, G_B, R)
    Bc = Bm.reshape(b, nc, Q, G_B, N_B)
    Cc = Cm.reshape(b, nc, Q, G_B, N_B)
    a_cum = jnp.cumsum(a, axis=2)
    tril = jnp.tril(jnp.ones((Q, Q), bool))[:, :, None, None]
    seg = a_cum[:, :, :, None] - a_cum[:, :, None, :]
    Lm = jnp.where(tril, jnp.exp(jnp.where(tril, seg, 0.0)), 0.0).astype(x.dtype)
    cb = jnp.einsum('bcign,bcjgn->bcijg', Cc, Bc)
    y_diag = jnp.einsum('bcijgr,bcjgrp->bcigrp', cb[..., None] * Lm, xdt)
    decay_end = jnp.exp(a_cum[:, :, -1:] - a_cum).astype(x.dtype)
    states = jnp.einsum('bcjgn,bcjgrp->bcgrpn', Bc, xdt * decay_end[..., None])
    chunk_decay = jnp.exp(a_cum[:, :, -1]).astype(x.dtype)

    def step(h, inp):
        st, dec = inp
        return h * dec[..., None, None] + st, h

    h_init = h0.reshape(b, G_B, R, HEADDIM_B, N_B).astype(x.dtype)
    h_fin, h_prev = lax.scan(step, h_init, (jnp.moveaxis(states, 1, 0), jnp.moveaxis(chunk_decay, 1, 0)))
    h_prev = jnp.moveaxis(h_prev, 0, 1)
    y_off = jnp.einsum('bcign,bcgrpn->bcigrp', Cc, h_prev) * jnp.exp(a_cum).astype(x.dtype)[..., None]
    y = (y_diag + y_off).reshape(b, nc * Q, H_B, HEADDIM_B)[:, :L]
    return y, h_fin.reshape(b, H_B, HEADDIM_B, N_B)


def gated_delta_rule(q, k, v, g, beta, S0):
    out_dtype = v.dtype
    b, L = q.shape[:2]
    Q = min(CHUNK_C, L)
    pad = (-L) % Q
    nc = (L + pad) // Q

    def prep(t):
        t = pad_seq(t.astype(jnp.float32), pad)
        t = t.reshape((b, nc, Q) + t.shape[2:])
        return jnp.moveaxis(t, 3, 1)

    q, k, v, g, beta = prep(q), prep(k), prep(v), prep(g), prep(beta)
    gc = jnp.cumsum(g, axis=-1)
    incl = jnp.tril(jnp.ones((Q, Q), bool))
    strict = jnp.tril(jnp.ones((Q, Q), bool), -1)
    decay = jnp.where(incl, jnp.exp(jnp.where(incl, gc[..., :, None] - gc[..., None, :], 0.0)), 0.0)
    kb = k * beta[..., None]
    M = jnp.where(strict, jnp.einsum('bhcid,bhcjd->bhcij', kb, k) * decay, 0.0)
    eye = jnp.eye(Q, dtype=jnp.float32)
    T = lax.linalg.triangular_solve(eye + M, jnp.broadcast_to(eye, M.shape), left_side=True, lower=True)
    u = T @ (v * beta[..., None])
    w = T @ (kb * jnp.exp(gc)[..., None])
    qk = jnp.einsum('bhcid,bhcjd->bhcij', q, k) * decay
    g_last = gc[..., -1]
    k_dec = k * jnp.exp(g_last[..., None] - gc)[..., None]
    q_dec = q * jnp.exp(gc)[..., None]

    def step(S, inp):
        u_c, w_c, qk_c, qd_c, kd_c, gl_c = inp
        v_new = u_c - w_c @ S
        o_c = qd_c @ S + qk_c @ v_new
        S = S * jnp.exp(gl_c)[..., None, None] + jnp.einsum('bhjd,bhje->bhde', kd_c, v_new)
        return S, o_c

    xs = (jnp.moveaxis(u, 2, 0), jnp.moveaxis(w, 2, 0), jnp.moveaxis(qk, 2, 0),
          jnp.moveaxis(q_dec, 2, 0), jnp.moveaxis(k_dec, 2, 0), jnp.moveaxis(g_last, 2, 0))
    S_fin, o = lax.scan(step, S0.astype(jnp.float32), xs)
    o = jnp.moveaxis(jnp.moveaxis(o, 0, 2), 1, 3).reshape(b, nc * Q, H_C, DV_C)[:, :L]
    return o.astype(out_dtype), S_fin.astype(S0.dtype)


def mixer_layer(x, attn_past, ssm_h0, conv_b_prev, delta_S0, conv_c_prev, lam_init, rel_table,
                norm_w, w_in, lam_q1, lam_k1, lam_q2, lam_k2, subln_w,
                conv_w_b, conv_bias_b, dt_bias_b, a_log_b, d_skip_b, norm_b_w,
                conv_w_c, dt_bias_c, a_log_c, norm_c_w,
                w_branch_a, w_branch_b, w_branch_c, w_out):
    b, L, _ = x.shape
    h = rmsnorm(x, norm_w)
    u = h @ w_in
    offs = np.cumsum(SPLIT_SIZES)[:-1].tolist()
    q_a, k_a, v_a, z_a, z_b, xbc_b, dt_b, qkv_c, z_c, a_c, b_c, gates = jnp.split(u, offs, axis=-1)

    lam = jnp.exp(jnp.sum(lam_q1 * lam_k1)) - jnp.exp(jnp.sum(lam_q2 * lam_k2)) + lam_init
    q_a = q_a.reshape(b, L, H_A, 2, DK_A)
    k_a = k_a.reshape(b, L, H_A, 2, DK_A)
    v_a = v_a.reshape(b, L, H_A, DV_A)
    if attn_past is None:
        o_a = diff_attn_prompt(q_a, k_a, v_a, rel_table, lam)
    else:
        o_a = diff_attn_sample(q_a, k_a, v_a, attn_past[0], attn_past[1], rel_table, lam)
    o_a = rmsnorm(o_a, subln_w) * (1.0 - lam_init)
    o_a = o_a.reshape(b, L, WIDTH_A) * jax.nn.silu(z_a)

    xbc, conv_b_state = causal_conv(xbc_b, conv_w_b, conv_bias_b, conv_b_prev)
    xbc = jax.nn.silu(xbc)
    x_b, B_b, C_b = jnp.split(xbc, [D_INNER_B, D_INNER_B + G_B * N_B], axis=-1)
    dt = jax.nn.softplus(dt_b + dt_bias_b)
    A = -jnp.exp(a_log_b)
    x_b = x_b.reshape(b, L, H_B, HEADDIM_B)
    y_b, ssm_h = ssd_scan(x_b, dt, A, B_b.reshape(b, L, G_B, N_B), C_b.reshape(b, L, G_B, N_B), ssm_h0)
    y_b = y_b + d_skip_b[:, None] * x_b
    o_b = rmsnorm(y_b.reshape(b, L, D_INNER_B) * jax.nn.silu(z_b), norm_b_w)

    qkv, conv_c_state = causal_conv(qkv_c, conv_w_c, None, conv_c_prev)
    qkv = jax.nn.silu(qkv)
    q_c, k_c, v_c = jnp.split(qkv, [H_C * DK_C, 2 * H_C * DK_C], axis=-1)
    q_c = l2norm(q_c.reshape(b, L, H_C, DK_C)) * (DK_C ** -0.5)
    k_c = l2norm(k_c.reshape(b, L, H_C, DK_C))
    v_c = v_c.reshape(b, L, H_C, DV_C)
    beta = jax.nn.sigmoid(b_c)
    g = -jnp.exp(a_log_c) * jax.nn.softplus(a_c + dt_bias_c)
    o_c, delta_S = gated_delta_rule(q_c, k_c, v_c, g, beta, delta_S0)
    o_c = rmsnorm(o_c, norm_c_w).reshape(b, L, WIDTH_C) * jax.nn.silu(z_c)

    g_a, g_b, g_c = jnp.split(jax.nn.sigmoid(gates), N_BRANCH, axis=-1)
    merged = g_a * (o_a @ w_branch_a) + g_b * (o_b @ w_branch_b) + g_c * (o_c @ w_branch_c)
    y = x + merged @ w_out
    new_state = (k_a.reshape(b, L, H_A, 2 * DK_A), v_a, ssm_h, conv_b_state, delta_S, conv_c_state)
    return y, new_state


def stack_layers(states, i):
    return jnp.stack([s[i] for s in states], axis=0)


def setup_inputs(seed: int = 0) -> dict:
    key = jax.random.key(seed)
    ks = jax.random.split(key, 40)
    f32 = jnp.float32
    n_pages = PAST_LEN // PAGE_SIZE
    n_used = DEC_BATCH * n_pages
    n_phys = n_used + (n_used + 3) // 4

    def nrm(k, shape, scale):
        return jax.random.normal(k, shape, f32) * scale

    def gain(k, shape):
        return 1.0 + 0.05 * jax.random.normal(k, shape, f32)

    def dt_bias_init(k, shape):
        dt = jnp.exp(jax.random.uniform(k, shape, f32) * (math.log(0.1) - math.log(0.001)) + math.log(0.001))
        return dt + jnp.log(-jnp.expm1(-dt))

    def a_log_init(k, shape):
        return jnp.log(jax.random.uniform(k, shape, f32, 1.0, 16.0))

    page_table = jax.random.permutation(ks[4], n_phys)[:n_used].reshape(DEC_BATCH, n_pages).astype(jnp.int32)
    return {
        'x_prompt': nrm(ks[0], (BATCH, SEQ, D_MODEL), 1.0),
        'x_sample': nrm(ks[1], (DEC_BATCH, DEC_SEQ, D_MODEL), 1.0),
        'cache_k': nrm(ks[2], (DEPTH, n_phys, PAGE_SIZE, H_A, 2 * DK_A), 1.0),
        'cache_v': nrm(ks[3], (DEPTH, n_phys, PAGE_SIZE, H_A, DV_A), 1.0),
        'page_table': page_table,
        'state_ssm': nrm(ks[5], (DEPTH, DEC_BATCH, H_B, HEADDIM_B, N_B), 0.5),
        'state_conv_ssm': nrm(ks[6], (DEPTH, DEC_BATCH, CONV_K - 1, CONV_DIM_B), 1.0),
        'state_delta': nrm(ks[7], (DEPTH, DEC_BATCH, H_C, DK_C, DV_C), 0.5),
        'state_conv_delta': nrm(ks[8], (DEPTH, DEC_BATCH, CONV_K - 1, CONV_DIM_C), 1.0),
        'rel_bias_table': nrm(ks[9], (NUM_BUCKETS, H_A), 0.5),
        'norm_w': gain(ks[10], (DEPTH, D_MODEL)),
        'w_in': nrm(ks[11], (DEPTH, D_MODEL, W_IN), D_MODEL ** -0.5),
        'lam_q1': nrm(ks[12], (DEPTH, DK_A), 0.1),
        'lam_k1': nrm(ks[13], (DEPTH, DK_A), 0.1),
        'lam_q2': nrm(ks[14], (DEPTH, DK_A), 0.1),
        'lam_k2': nrm(ks[15], (DEPTH, DK_A), 0.1),
        'subln_w': gain(ks[16], (DEPTH, DV_A)),
        'conv_w_b': nrm(ks[17], (DEPTH, CONV_K, CONV_DIM_B), CONV_K ** -0.5),
        'conv_bias_b': nrm(ks[18], (DEPTH, CONV_DIM_B), 0.02),
        'dt_bias_b': dt_bias_init(ks[19], (DEPTH, H_B)),
        'a_log_b': a_log_init(ks[20], (DEPTH, H_B)),
        'd_skip_b': gain(ks[21], (DEPTH, H_B)),
        'norm_b_w': gain(ks[22], (DEPTH, D_INNER_B)),
        'conv_w_c': nrm(ks[23], (DEPTH, CONV_K, CONV_DIM_C), CONV_K ** -0.5),
        'dt_bias_c': dt_bias_init(ks[24], (DEPTH, H_C)),
        'a_log_c': a_log_init(ks[25], (DEPTH, H_C)),
        'norm_c_w': gain(ks[26], (DEPTH, DV_C)),
        'w_branch_a': nrm(ks[27], (DEPTH, WIDTH_A, D_MODEL), WIDTH_A ** -0.5),
        'w_branch_b': nrm(ks[28], (DEPTH, D_INNER_B, D_MODEL), D_INNER_B ** -0.5),
        'w_branch_c': nrm(ks[29], (DEPTH, WIDTH_C, D_MODEL), WIDTH_C ** -0.5),
        'w_out': nrm(ks[30], (DEPTH, D_MODEL, D_MODEL), D_MODEL ** -0.5),
        'final_norm_w': gain(ks[31], (D_MODEL,)),
    }


def reference(x_prompt, x_sample, cache_k, cache_v, page_table, state_ssm, state_conv_ssm, state_delta,
              state_conv_delta, rel_bias_table, norm_w, w_in, lam_q1, lam_k1, lam_q2, lam_k2, subln_w,
              conv_w_b, conv_bias_b, dt_bias_b, a_log_b, d_skip_b, norm_b_w, conv_w_c, dt_bias_c, a_log_c,
              norm_c_w, w_branch_a, w_branch_b, w_branch_c, w_out, final_norm_w):
    bp = x_prompt.shape[0]
    bs = x_sample.shape[0]
    past_len = page_table.shape[1] * PAGE_SIZE
    dtp = x_prompt.dtype
    zero_ssm = jnp.zeros((bp, H_B, HEADDIM_B, N_B), dtp)
    zero_conv_b = jnp.zeros((bp, CONV_K - 1, CONV_DIM_B), dtp)
    zero_delta = jnp.zeros((bp, H_C, DK_C, DV_C), dtp)
    zero_conv_c = jnp.zeros((bp, CONV_K - 1, CONV_DIM_C), dtp)
    xp, xs = x_prompt, x_sample
    st_p, st_s = [], []
    for l in range(DEPTH):
        lam_init = 0.8 - 0.6 * math.exp(-0.3 * l)
        layer_w = (norm_w[l], w_in[l], lam_q1[l], lam_k1[l], lam_q2[l], lam_k2[l], subln_w[l],
                   conv_w_b[l], conv_bias_b[l], dt_bias_b[l], a_log_b[l], d_skip_b[l], norm_b_w[l],
                   conv_w_c[l], dt_bias_c[l], a_log_c[l], norm_c_w[l],
                   w_branch_a[l], w_branch_b[l], w_branch_c[l], w_out[l])
        xp, new_p = mixer_layer(xp, None, zero_ssm, zero_conv_b, zero_delta, zero_conv_c,
                                lam_init, rel_bias_table, *layer_w)
        past_k = cache_k[l][page_table].reshape(bs, past_len, H_A, 2, DK_A)
        past_v = cache_v[l][page_table].reshape(bs, past_len, H_A, DV_A)
        xs, new_s = mixer_layer(xs, (past_k, past_v), state_ssm[l], state_conv_ssm[l], state_delta[l],
                                state_conv_delta[l], lam_init, rel_bias_table, *layer_w)
        st_p.append(new_p)
        st_s.append(new_s)
    y_prompt = rmsnorm(xp, final_norm_w)
    y_sample = rmsnorm(xs, final_norm_w)
    k_prompt, k_sample = stack_layers(st_p, 0), stack_layers(st_s, 0)
    v_prompt, v_sample = stack_layers(st_p, 1), stack_layers(st_s, 1)
    ssm_prompt, ssm_sample = stack_layers(st_p, 2), stack_layers(st_s, 2)
    conv_ssm_prompt, conv_ssm_sample = stack_layers(st_p, 3), stack_layers(st_s, 3)
    delta_prompt, delta_sample = stack_layers(st_p, 4), stack_layers(st_s, 4)
    conv_delta_prompt, conv_delta_sample = stack_layers(st_p, 5), stack_layers(st_s, 5)
    return (y_prompt, y_sample, k_prompt, v_prompt, k_sample, v_sample, ssm_prompt, ssm_sample,
            conv_ssm_prompt, conv_ssm_sample, delta_prompt, delta_sample, conv_delta_prompt, conv_delta_sample)
```

```python
import functools
import math

import numpy as np
import jax
import jax.numpy as jnp
from jax import lax
from jax.experimental import pallas as pl
from jax.experimental.pallas import tpu as pltpu

F32 = jnp.float32
BF16 = jnp.bfloat16
HIGHEST = lax.Precision.HIGHEST

DK_A = 64
DV_A = 128
NUM_BUCKETS = 32
MAX_DISTANCE = 128
PAGE_SIZE = 128
P_B = 64
G_B = 4
N_B = 128
DK_C = 128
CONV_K = 4
EPS = 1e-6
NEG = -1e30

LANES = 128
VMEM_LIMIT = 56 * 1024 * 1024

ATTN_BLOCK = 512
SSD_CHUNK = 128
GDN_CHUNK = 64
GDN_UNROLL = 4


def _params(*sem):
    return pltpu.CompilerParams(dimension_semantics=sem, vmem_limit_bytes=VMEM_LIMIT)


def _pick(n, cands):
    for c in cands:
        if n % c == 0:
            return c
    return n


def _silu(x):
    return x * jax.nn.sigmoid(x)


def _softplus(x):
    return jnp.maximum(x, 0.0) + jnp.log1p(jnp.exp(-jnp.abs(x)))


def _dot(a, b, precision=None):
    return jnp.dot(a, b, preferred_element_type=F32, precision=precision)


def _dot_nt(a, b, precision=None):
    return lax.dot_general(a, b, (((1,), (1,)), ((), ())), preferred_element_type=F32, precision=precision)


def _dot_tn(a, b, precision=None):
    return lax.dot_general(a, b, (((0,), (0,)), ((), ())), preferred_element_type=F32, precision=precision)


def _rmsnorm_body(x_ref, w_ref, o_ref):
    x = x_ref[...]
    r = lax.rsqrt(jnp.mean(x * x, axis=-1, keepdims=True) + EPS)
    o_ref[...] = ((x * r) * w_ref[...]).astype(o_ref.dtype)


def rmsnorm(x, w, out_dtype):
    m, d = x.shape
    bm = _pick(m, (256, 128, 64, 32, 16, 8))
    return pl.pallas_call(
        _rmsnorm_body,
        grid=(m // bm,),
        in_specs=[pl.BlockSpec((bm, d), lambda i: (i, 0)), pl.BlockSpec((1, d), lambda i: (0, 0))],
        out_specs=pl.BlockSpec((bm, d), lambda i: (i, 0)),
        out_shape=jax.ShapeDtypeStruct((m, d), out_dtype),
        compiler_params=_params("parallel"),
        name="rmsnorm",
    )(x, w.reshape(1, d))


def _mm_body(x_ref, w_ref, o_ref):
    o_ref[...] = _dot(x_ref[...], w_ref[...]).astype(o_ref.dtype)


def matmul(x, w, out_dtype=F32):
    m, k = x.shape
    n = w.shape[1]
    bm = _pick(m, (1024, 512, 256, 128, 64, 32, 16))
    bn = _pick(n, (512, 384, 256, 128))
    return pl.pallas_call(
        _mm_body,
        grid=(m // bm, n // bn),
        in_specs=[pl.BlockSpec((bm, k), lambda i, j: (i, 0)), pl.BlockSpec((k, bn), lambda i, j: (0, j))],
        out_specs=pl.BlockSpec((bm, bn), lambda i, j: (i, j)),
        out_shape=jax.ShapeDtypeStruct((m, n), out_dtype),
        compiler_params=_params("parallel", "arbitrary"),
        name="matmul",
    )(x, w)


def _merge_body(oa_ref, ob_ref, oc_ref, wa_ref, wb_ref, wc_ref, ga_ref, gb_ref, gc_ref, o_ref):
    acc = jax.nn.sigmoid(ga_ref[...]) * _dot(oa_ref[...], wa_ref[...])
    acc = acc + jax.nn.sigmoid(gb_ref[...]) * _dot(ob_ref[...], wb_ref[...])
    acc = acc + jax.nn.sigmoid(gc_ref[...]) * _dot(oc_ref[...], wc_ref[...])
    o_ref[...] = acc.astype(o_ref.dtype)


def merge_branches(o_a, o_b, o_c, w_a, w_b, w_c, u, gate_off):
    m, hd = o_a.shape
    d = w_a.shape[1]
    bm = _pick(m, (512, 256, 128, 64, 32, 16))
    bn = _pick(math.gcd(d, gate_off), (512, 256, 128))
    g0 = gate_off // bn
    nd = d // bn
    row = pl.BlockSpec((bm, hd), lambda i, j: (i, 0))
    wsp = pl.BlockSpec((hd, bn), lambda i, j: (0, j))

    def gate(t):
        return pl.BlockSpec((bm, bn), lambda i, j: (i, g0 + t * nd + j))

    return pl.pallas_call(
        _merge_body,
        grid=(m // bm, nd),
        in_specs=[row, row, row, wsp, wsp, wsp, gate(0), gate(1), gate(2)],
        out_specs=pl.BlockSpec((bm, bn), lambda i, j: (i, j)),
        out_shape=jax.ShapeDtypeStruct((m, d), BF16),
        compiler_params=_params("parallel", "arbitrary"),
        name="merge",
    )(o_a, o_b, o_c, w_a, w_b, w_c, u, u, u)


def _outproj_body(m_ref, w_ref, x_ref, o_ref):
    o_ref[...] = x_ref[...] + _dot(m_ref[...], w_ref[...])


def out_projection(merged, w_out, x):
    m, d = merged.shape
    bm = _pick(m, (1024, 512, 256, 128, 64, 32, 16))
    bn = _pick(d, (512, 256, 128))
    return pl.pallas_call(
        _outproj_body,
        grid=(m // bm, d // bn),
        in_specs=[pl.BlockSpec((bm, d), lambda i, j: (i, 0)), pl.BlockSpec((d, bn), lambda i, j: (0, j)),
                  pl.BlockSpec((bm, bn), lambda i, j: (i, j))],
        out_specs=pl.BlockSpec((bm, bn), lambda i, j: (i, j)),
        out_shape=jax.ShapeDtypeStruct((m, d), F32),
        compiler_params=_params("parallel", "arbitrary"),
        name="outproj",
    )(merged, w_out, x)


def _bucket_changes():
    max_exact = NUM_BUCKETS // 2
    n = np.arange(0, MAX_DISTANCE + 1)
    nf = np.maximum(n, 1).astype(np.float32)
    large = max_exact + (np.log(nf / np.float32(max_exact)) / np.float32(math.log(MAX_DISTANCE / max_exact))
                         * np.float32(NUM_BUCKETS - max_exact)).astype(np.int32)
    bucket = np.where(n < max_exact, n, np.minimum(large, NUM_BUCKETS - 1))
    bucket[MAX_DISTANCE] = NUM_BUCKETS - 1
    changes = [(0, int(bucket[0]))]
    for d in range(1, MAX_DISTANCE + 1):
        if bucket[d] != bucket[d - 1]:
            changes.append((d, int(bucket[d])))
    return changes


def _bias_tiles_body(tbl_ref, o_ref, *, t, n_heads):
    h = pl.program_id(0)
    i = lax.broadcasted_iota(jnp.int32, (t, t), 0)
    j = lax.broadcasted_iota(jnp.int32, (t, t), 1)
    last = tbl_ref[(NUM_BUCKETS - 1) * n_heads + h]
    changes = _bucket_changes()
    for tile, off in ((0, 0), (1, t)):
        d = i - j + off
        val = jnp.full((t, t), tbl_ref[changes[0][1] * n_heads + h] - last, F32)
        for ds, b in changes[1:]:
            val = jnp.where(d >= ds, tbl_ref[b * n_heads + h] - last, val)
        if off == 0:
            val = jnp.where(j <= i, val, NEG)
        o_ref[tile] = val


def bias_tiles(table, t):
    n_heads = table.shape[1]
    return pl.pallas_call(
        functools.partial(_bias_tiles_body, t=t, n_heads=n_heads),
        grid=(n_heads,),
        in_specs=[pl.BlockSpec(memory_space=pltpu.SMEM)],
        out_specs=pl.BlockSpec((None, 2, t, t), lambda h: (h, 0, 0, 0)),
        out_shape=jax.ShapeDtypeStruct((n_heads, 2, t, t), F32),
        compiler_params=_params("parallel"),
        name="bias_tiles",
    )(table.reshape(-1))


def _attn_body(scal_ref, q_ref, k_ref, v_ref, z_ref, bias_ref, sw_ref, o_ref, kb_ref, vb_ref, *, t):
    qi = pl.program_id(2)

    @pl.when(qi == 0)
    def _():
        kb_ref[...] = k_ref[...].astype(BF16)
        vb_ref[...] = v_ref[...].astype(BF16)

    q = q_ref[...] * (DK_A ** -0.5)
    lane = lax.broadcasted_iota(jnp.int32, q.shape, 1)
    qq = jnp.concatenate([jnp.where(lane < DK_A, q, 0.0), jnp.where(lane >= DK_A, q, 0.0)], axis=0).astype(BF16)

    def step(kc, carry, bias):
        m, l, acc = carry
        start = pl.multiple_of(kc * t, t)
        s = _dot_nt(qq, kb_ref[pl.ds(start, t), :])
        if bias is not None:
            s = (s.reshape(2, t, t) + bias[None]).reshape(2 * t, t)
        m_new = jnp.maximum(m, jnp.max(s, axis=-1, keepdims=True))
        alpha = jnp.exp(m - m_new)
        p = jnp.exp(s - m_new)
        l = alpha * l + jnp.sum(p, axis=-1, keepdims=True)
        acc = alpha * acc + _dot(p.astype(BF16), vb_ref[pl.ds(start, t), :])
        return m_new, l, acc

    carry = (jnp.full((2 * t, 1), NEG, F32), jnp.zeros((2 * t, 1), F32), jnp.zeros((2 * t, DV_A), F32))
    carry = lax.fori_loop(0, jnp.maximum(qi - 1, 0), lambda kc, c: step(kc, c, None), carry)
    carry = lax.fori_loop(jnp.maximum(qi - 1, 0), qi, lambda kc, c: step(kc, c, bias_ref[1]), carry)
    _, l, acc = step(qi, carry, bias_ref[0])

    o = acc / l
    o = o[:t] - scal_ref[0] * o[t:]
    r = lax.rsqrt(jnp.mean(o * o, axis=-1, keepdims=True) + EPS)
    o = (o * r) * sw_ref[...] * scal_ref[1]
    o_ref[...] = (o * _silu(z_ref[...])).astype(o_ref.dtype)


def prompt_attention(u, bias, subln_w, scal, batch, seq, hd, t):
    n_heads = hd // DV_A
    nq = seq // t
    return pl.pallas_call(
        functools.partial(_attn_body, t=t),
        grid=(batch, n_heads, nq),
        in_specs=[
            pl.BlockSpec(memory_space=pltpu.SMEM),
            pl.BlockSpec((t, DV_A), lambda b, h, qi: (b * nq + qi, h)),
            pl.BlockSpec((seq, DV_A), lambda b, h, qi: (b, n_heads + h)),
            pl.BlockSpec((seq, DV_A), lambda b, h, qi: (b, 2 * n_heads + h)),
            pl.BlockSpec((t, DV_A), lambda b, h, qi: (b * nq + qi, 3 * n_heads + h)),
            pl.BlockSpec((None, 2, t, t), lambda b, h, qi: (h, 0, 0, 0)),
            pl.BlockSpec((1, DV_A), lambda b, h, qi: (0, 0)),
        ],
        out_specs=pl.BlockSpec((t, DV_A), lambda b, h, qi: (b * nq + qi, h)),
        out_shape=jax.ShapeDtypeStruct((batch * seq, hd), BF16),
        scratch_shapes=[pltpu.VMEM((seq, DV_A), BF16), pltpu.VMEM((seq, DV_A), BF16)],
        compiler_params=_params("parallel", "parallel", "arbitrary"),
        name="prompt_attn",
    )(scal, u, u, u, u, bias, subln_w.reshape(1, DV_A))


def _conv_body(x_ref, w_ref, b_ref, o_ref):
    x = x_ref[...]
    row = lax.broadcasted_iota(jnp.int32, x.shape, 0)
    acc = x * w_ref[CONV_K - 1:CONV_K, :]
    for k in range(1, CONV_K):
        shifted = jnp.where(row >= k, pltpu.roll(x, k, axis=0), 0.0)
        acc = acc + shifted * w_ref[CONV_K - 1 - k:CONV_K - k, :]
    acc = acc + b_ref[...]
    o_ref[...] = _silu(acc)


def prompt_conv(u, col_off, width, w, bias, batch, seq):
    cw = _pick(math.gcd(col_off, width), (512, 256, 128))
    c0 = col_off // cw
    return pl.pallas_call(
        _conv_body,
        grid=(batch, width // cw),
        in_specs=[pl.BlockSpec((seq, cw), lambda b, c: (b, c0 + c)),
                  pl.BlockSpec((CONV_K, cw), lambda b, c: (0, c)),
                  pl.BlockSpec((1, cw), lambda b, c: (0, c))],
        out_specs=pl.BlockSpec((seq, cw), lambda b, c: (b, c)),
        out_shape=jax.ShapeDtypeStruct((batch * seq, width), F32),
        compiler_params=_params("parallel", "parallel"),
        name="prompt_conv",
    )(u, w, bias.reshape(1, width))


def _sample_conv_body(prev_ref, x_ref, w_ref, b_ref, y_ref, st_ref):
    x = x_ref[...]
    acc = x * w_ref[CONV_K - 1:CONV_K, :]
    for i in range(CONV_K - 1):
        acc = acc + prev_ref[i] * w_ref[i:i + 1, :]
    y_ref[...] = _silu(acc + b_ref[...])
    for i in range(CONV_K - 2):
        st_ref[i] = prev_ref[i + 1]
    st_ref[CONV_K - 2] = x


def sample_conv(prev, x, w, bias):
    rows, c = x.shape
    prev_t = jnp.transpose(prev, (1, 0, 2))
    cw = _pick(c, (1024, 512, 256, 128))
    y, st = pl.pallas_call(
        _sample_conv_body,
        grid=(c // cw,),
        in_specs=[pl.BlockSpec((CONV_K - 1, rows, cw), lambda j: (0, 0, j)),
                  pl.BlockSpec((rows, cw), lambda j: (0, j)),
                  pl.BlockSpec((CONV_K, cw), lambda j: (0, j)),
                  pl.BlockSpec((1, cw), lambda j: (0, j))],
        out_specs=[pl.BlockSpec((rows, cw), lambda j: (0, j)),
                   pl.BlockSpec((CONV_K - 1, rows, cw), lambda j: (0, 0, j))],
        out_shape=[jax.ShapeDtypeStruct((rows, c), F32), jax.ShapeDtypeStruct((CONV_K - 1, rows, c), F32)],
        compiler_params=_params("parallel"),
        name="sample_conv",
    )(prev_t, x, w, bias.reshape(1, c))
    return y, jnp.transpose(st, (1, 0, 2))


def _ssd_body(x_ref, us_ref, z_ref, dtb_ref, alog_ref, dskip_ref, nw_ref, o_ref, h_ref, *, t, hd, n_heads):
    c = pl.program_id(1)

    @pl.when(c == 0)
    def _():
        h_ref[...] = jnp.zeros_like(h_ref)

    heads_per_group = n_heads // G_B
    dt = _softplus(us_ref[:, 0:n_heads] + dtb_ref[...])
    a = dt * (-jnp.exp(alog_ref[...]))
    ri = lax.broadcasted_iota(jnp.int32, (t, t), 0)
    ci = lax.broadcasted_iota(jnp.int32, (t, t), 1)
    lower = ci <= ri
    a_col = _dot(lower.astype(F32), a, HIGHEST)
    a_row = _dot_tn(a, (ri <= ci).astype(F32), HIGHEST)
    a_last = a_col[t - 1:t, :]
    e_col = jnp.exp(a_col)
    e_end = jnp.exp(a_last - a_col)
    e_last = jnp.exp(a_last)
    lane = lax.broadcasted_iota(jnp.int32, (t, LANES), 1)
    lo = lane < P_B
    row_lo = lax.broadcasted_iota(jnp.int32, (2 * P_B, N_B), 0) < P_B
    dskip = dskip_ref[...]

    cb = {}
    ys = []
    for j in range(n_heads // 2):
        ha, hb = 2 * j, 2 * j + 1
        g = ha // heads_per_group
        bg = x_ref[:, hd + g * N_B:hd + (g + 1) * N_B].astype(BF16)
        cg = x_ref[:, hd + (G_B + g) * N_B:hd + (G_B + g + 1) * N_B].astype(BF16)
        if g not in cb:
            cb[g] = _dot_nt(cg, bg)
        x = x_ref[:, j * LANES:(j + 1) * LANES]
        xdt = x * jnp.where(lo, dt[:, ha:ha + 1], dt[:, hb:hb + 1])
        xdt_b = xdt.astype(BF16)
        y = None
        for hh, keep in ((ha, lo), (hb, ~lo)):
            seg = a_col[:, hh:hh + 1] - a_row[hh:hh + 1, :]
            w = (cb[g] * jnp.exp(jnp.where(lower, seg, NEG))).astype(BF16)
            part = _dot(w, jnp.where(keep, xdt_b, jnp.zeros_like(xdt_b)))
            y = part if y is None else y + part
        h_prev = h_ref[ha:hb + 1].reshape(2 * P_B, N_B)
        y = y + _dot_nt(cg, h_prev.astype(BF16)) * jnp.where(lo, e_col[:, ha:ha + 1], e_col[:, hb:hb + 1])
        xdec = (xdt * jnp.where(lo, e_end[:, ha:ha + 1], e_end[:, hb:hb + 1])).astype(BF16)
        st = _dot_tn(xdec, bg)
        h_new = h_prev * jnp.where(row_lo, e_last[:, ha:ha + 1], e_last[:, hb:hb + 1]) + st
        h_ref[ha:hb + 1] = h_new.reshape(2, P_B, N_B)
        y = y + x * jnp.where(lo[:1], dskip[:, ha:ha + 1], dskip[:, hb:hb + 1])
        ys.append(y)
    y = jnp.concatenate(ys, axis=1)
    y = y * _silu(z_ref[...])
    r = lax.rsqrt(jnp.mean(y * y, axis=-1, keepdims=True) + EPS)
    o_ref[...] = ((y * r) * nw_ref[...]).astype(o_ref.dtype)


def prompt_ssd(xc, us, u, z_blk, dt_bias, a_log, d_skip, norm_w, batch, seq, hd):
    n_heads = hd // P_B
    t = min(SSD_CHUNK, seq)
    nc = seq // t
    cdim = xc.shape[1]
    vec = pl.BlockSpec((1, n_heads), lambda b, c: (0, 0))
    return pl.pallas_call(
        functools.partial(_ssd_body, t=t, hd=hd, n_heads=n_heads),
        grid=(batch, nc),
        in_specs=[pl.BlockSpec((t, cdim), lambda b, c: (b * nc + c, 0)),
                  pl.BlockSpec((t, LANES), lambda b, c: (b * nc + c, 0)),
                  pl.BlockSpec((t, hd), lambda b, c: (b * nc + c, z_blk)),
                  vec, vec, vec,
                  pl.BlockSpec((1, hd), lambda b, c: (0, 0))],
        out_specs=[pl.BlockSpec((t, hd), lambda b, c: (b * nc + c, 0)),
                   pl.BlockSpec((None, n_heads, P_B, N_B), lambda b, c: (b, 0, 0, 0))],
        out_shape=[jax.ShapeDtypeStruct((batch * seq, hd), BF16),
                   jax.ShapeDtypeStruct((batch, n_heads, P_B, N_B), F32)],
        compiler_params=_params("parallel", "arbitrary"),
        name="prompt_ssd",
    )(xc, us, u, dt_bias.reshape(1, -1), a_log.reshape(1, -1), d_skip.reshape(1, -1), norm_w.reshape(1, hd))


def _unit_lower_inverse(m, n):
    ri = lax.broadcasted_iota(jnp.int32, (n, n), 0)
    ci = lax.broadcasted_iota(jnp.int32, (n, n), 1)
    x = jnp.where(ri == ci, 1.0, 0.0) - m
    p = m
    k = 2
    while k < 2 * n:
        p = _dot(p, p, HIGHEST)
        x = x + _dot(x, p, HIGHEST)
        k *= 2
    return x


def _l2norm(x):
    return x * lax.rsqrt(jnp.sum(x * x, axis=-1, keepdims=True) + EPS)


def _gdn_body(alog_ref, dtb_ref, q_ref, k_ref, v_ref, z_ref, us_ref, nw_ref, o_ref, s_ref,
              u_s, w_s, qk_s, qd_s, kd_s, gl_s, *, tc, nc, a_lane, b_lane):
    h = pl.program_id(1)
    neg_a = -jnp.exp(jnp.full((1, LANES), alog_ref[h], F32))
    dtb = dtb_ref[h]
    sel_r = lax.broadcasted_iota(jnp.int32, (LANES, LANES), 0)
    pick_a = (sel_r == a_lane + h).astype(F32)
    pick_b = (sel_r == b_lane + h).astype(F32)
    ri = lax.broadcasted_iota(jnp.int32, (tc, tc), 0)
    ci = lax.broadcasted_iota(jnp.int32, (tc, tc), 1)
    incl = ci <= ri
    strict = ci < ri
    tri = incl.astype(F32)
    tri_t = (ri <= ci).astype(F32)

    def prepare(cidx):
        rows = pl.ds(pl.multiple_of(cidx * tc, tc), tc)
        us = us_ref[rows, :]
        g = neg_a * _softplus(_dot(us, pick_a, HIGHEST) + dtb)
        beta = jax.nn.sigmoid(_dot(us, pick_b, HIGHEST))
        gc = _dot(tri, g, HIGHEST)
        gc_t = _dot_tn(g, tri_t, HIGHEST)
        decay = jnp.exp(jnp.where(incl, gc[:, :tc] - gc_t[:tc, :], NEG))
        q = _l2norm(q_ref[rows, :]) * (DK_C ** -0.5)
        k = _l2norm(k_ref[rows, :])
        v = v_ref[rows, :]
        kb = k * beta
        k_b = k.astype(BF16)
        m = jnp.where(strict, _dot_nt(kb.astype(BF16), k_b) * decay, 0.0)
        t_inv = _unit_lower_inverse(m, tc).astype(BF16)
        e_gc = jnp.exp(gc)
        u_s[rows, :] = _dot(t_inv, (v * beta).astype(BF16))
        w_s[rows, :] = _dot(t_inv, (kb * e_gc).astype(BF16)).astype(BF16)
        qk_s[rows, :] = (_dot_nt(q.astype(BF16), k_b) * decay).astype(BF16)
        g_last = gc[tc - 1:tc, :]
        qd_s[rows, :] = (q * e_gc).astype(BF16)
        kd_s[rows, :] = (k * jnp.exp(g_last - gc)).astype(BF16)
        gl_s[pl.ds(cidx, 1), :] = jnp.exp(g_last)

    def prepare_group(i, carry):
        for r in range(GDN_UNROLL):
            prepare(i * GDN_UNROLL + r)
        return carry

    lax.fori_loop(0, nc // GDN_UNROLL, prepare_group, 0)
    for r in range(nc % GDN_UNROLL):
        prepare((nc // GDN_UNROLL) * GDN_UNROLL + r)

    def recur(cidx, s):
        rows = pl.ds(pl.multiple_of(cidx * tc, tc), tc)
        s_b = s.astype(BF16)
        v_new = u_s[rows, :] - _dot(w_s[rows, :], s_b)
        v_b = v_new.astype(BF16)
        o = _dot(qd_s[rows, :], s_b) + _dot(qk_s[rows, :], v_b)
        r = lax.rsqrt(jnp.mean(o * o, axis=-1, keepdims=True) + EPS)
        o_ref[rows, :] = ((o * r) * nw_ref[...] * _silu(z_ref[rows, :])).astype(o_ref.dtype)
        return s * gl_s[pl.ds(cidx, 1), :] + _dot_tn(kd_s[rows, :], v_b)

    s_ref[...] = lax.fori_loop(0, nc, recur, jnp.zeros((DK_C, DK_C), F32))


def prompt_gdn(xc, us, u, z_blk0, a_log, dt_bias, norm_w, batch, seq, hd, a_lane, b_lane):
    n_heads = hd // DK_C
    tc = min(GDN_CHUNK, seq)
    nc = seq // tc
    smem = pl.BlockSpec(memory_space=pltpu.SMEM)
    head = lambda off: pl.BlockSpec((seq, DK_C), lambda b, h: (b, off + h))
    return pl.pallas_call(
        functools.partial(_gdn_body, tc=tc, nc=nc, a_lane=a_lane, b_lane=b_lane),
        grid=(batch, n_heads),
        in_specs=[smem, smem, head(0), head(n_heads), head(2 * n_heads), head(z_blk0),
                  pl.BlockSpec((seq, LANES), lambda b, h: (b, 0)),
                  pl.BlockSpec((1, DK_C), lambda b, h: (0, 0))],
        out_specs=[pl.BlockSpec((seq, DK_C), lambda b, h: (b, h)),
                   pl.BlockSpec((None, None, DK_C, DK_C), lambda b, h: (b, h, 0, 0))],
        out_shape=[jax.ShapeDtypeStruct((batch * seq, hd), BF16),
                   jax.ShapeDtypeStruct((batch, n_heads, DK_C, DK_C), F32)],
        scratch_shapes=[pltpu.VMEM((seq, DK_C), F32), pltpu.VMEM((seq, DK_C), BF16),
                        pltpu.VMEM((seq, tc), BF16), pltpu.VMEM((seq, DK_C), BF16),
                        pltpu.VMEM((seq, DK_C), BF16), pltpu.VMEM((nc, LANES), F32)],
        compiler_params=_params("parallel", "parallel"),
        name="prompt_gdn",
    )(a_log, dt_bias, xc, xc, xc, u, us, norm_w.reshape(1, DK_C))


def _sample_attn_body(pt_ref, scal_ref, q_ref, kn_ref, vn_ref, z_ref, tbl_ref, sw_ref, k_ref, v_ref, o_ref,
                      m_s, l_s, acc_s, *, n_heads, n_pages):
    p = pl.program_id(1)
    rows = 2 * n_heads
    cols = PAGE_SIZE * n_heads

    @pl.when(p == 0)
    def _():
        m_s[...] = jnp.full_like(m_s, NEG)
        l_s[...] = jnp.zeros_like(l_s)
        acc_s[...] = jnp.zeros_like(acc_s)

    q = q_ref[...] * (DK_A ** -0.5)
    lane = lax.broadcasted_iota(jnp.int32, q.shape, 1)
    qq = jnp.concatenate([jnp.where(lane < DK_A, q, 0.0), jnp.where(lane >= DK_A, q, 0.0)], axis=0).astype(BF16)
    k2 = k_ref[...].reshape(cols, DV_A).astype(BF16)
    v2 = v_ref[...].reshape(cols, DV_A).astype(BF16)
    s = _dot_nt(qq, k2)
    r_i = lax.broadcasted_iota(jnp.int32, (rows, cols), 0)
    c_i = lax.broadcasted_iota(jnp.int32, (rows, cols), 1)
    own = (c_i % n_heads) == (r_i % n_heads)
    last = tbl_ref[:, NUM_BUCKETS - 1:NUM_BUCKETS]
    changes = _bucket_changes()

    def add_bias(s):
        d = PAGE_SIZE - c_i // n_heads
        val = jnp.broadcast_to(tbl_ref[:, changes[0][1]:changes[0][1] + 1] - last, (rows, cols))
        for ds, b in changes[1:]:
            val = jnp.where(d >= ds, tbl_ref[:, b:b + 1] - last, val)
        return s + val

    s = lax.cond(p == n_pages - 1, add_bias, lambda s: s, s)
    s = jnp.where(own, s, NEG)
    m_new = jnp.maximum(m_s[...], jnp.max(s, axis=-1, keepdims=True))
    alpha = jnp.exp(m_s[...] - m_new)
    pr = jnp.where(own, jnp.exp(s - m_new), 0.0)
    l_s[...] = alpha * l_s[...] + jnp.sum(pr, axis=-1, keepdims=True)
    acc_s[...] = alpha * acc_s[...] + _dot(pr.astype(BF16), v2)
    m_s[...] = m_new

    @pl.when(p == n_pages - 1)
    def _():
        kn = jnp.concatenate([kn_ref[...], kn_ref[...]], axis=0).astype(BF16).astype(F32)
        vn = jnp.concatenate([vn_ref[...], vn_ref[...]], axis=0).astype(BF16).astype(F32)
        s_new = jnp.sum(qq.astype(F32) * kn, axis=-1, keepdims=True) + (tbl_ref[:, changes[0][1]:changes[0][1] + 1] - last)
        m_fin = jnp.maximum(m_s[...], s_new)
        a2 = jnp.exp(m_s[...] - m_fin)
        p_new = jnp.exp(s_new - m_fin)
        l = a2 * l_s[...] + p_new
        acc = a2 * acc_s[...] + p_new.astype(BF16).astype(F32) * vn
        o = acc / l
        o = o[:n_heads] - scal_ref[0] * o[n_heads:]
        r = lax.rsqrt(jnp.mean(o * o, axis=-1, keepdims=True) + EPS)
        o = (o * r) * sw_ref[...] * scal_ref[1]
        o_ref[...] = (o * _silu(z_ref[...])).astype(o_ref.dtype)


def sample_attention(q, k_new, v_new, z, cache_k, cache_v, layer, page_table, table2, subln_w, scal):
    batch, n_heads, _ = q.shape
    n_pages = page_table.shape[1]
    per_seq = pl.BlockSpec((None, n_heads, DV_A), lambda b, p, pt: (b, 0, 0))
    page = pl.BlockSpec((None, None, PAGE_SIZE, n_heads, DV_A), lambda b, p, pt: (layer, pt[b * n_pages + p], 0, 0, 0))
    grid_spec = pltpu.PrefetchScalarGridSpec(
        num_scalar_prefetch=1,
        grid=(batch, n_pages),
        in_specs=[pl.BlockSpec(memory_space=pltpu.SMEM), per_seq, per_seq, per_seq, per_seq,
                  pl.BlockSpec((2 * n_heads, NUM_BUCKETS), lambda b, p, pt: (0, 0)),
                  pl.BlockSpec((1, DV_A), lambda b, p, pt: (0, 0)),
                  page, page],
        out_specs=per_seq,
        scratch_shapes=[pltpu.VMEM((2 * n_heads, 1), F32), pltpu.VMEM((2 * n_heads, 1), F32),
                        pltpu.VMEM((2 * n_heads, DV_A), F32)],
    )
    return pl.pallas_call(
        functools.partial(_sample_attn_body, n_heads=n_heads, n_pages=n_pages),
        grid_spec=grid_spec,
        out_shape=jax.ShapeDtypeStruct((batch, n_heads, DV_A), BF16),
        compiler_params=_params("parallel", "arbitrary"),
        name="sample_attn",
    )(page_table.reshape(-1), scal, q, k_new, v_new, z, table2, subln_w.reshape(1, DV_A), cache_k, cache_v)


def _sample_ssd_body(x_ref, us_ref, z_ref, h0_ref, dtb_ref, alog_ref, dskip_ref, nw_ref, o_ref, h_ref, *, hd, n_heads):
    heads_per_group = n_heads // G_B
    dt = _softplus(us_ref[:, 0:n_heads] + dtb_ref[...])
    e_a = jnp.exp(dt * (-jnp.exp(alog_ref[...])))
    lane = lax.broadcasted_iota(jnp.int32, (1, LANES), 1)
    lo = lane < P_B
    row_lo = lax.broadcasted_iota(jnp.int32, (2 * P_B, N_B), 0) < P_B
    first = lax.broadcasted_iota(jnp.int32, (8, LANES), 0) == 0
    dskip = dskip_ref[...]
    ys = []
    for j in range(n_heads // 2):
        ha, hb = 2 * j, 2 * j + 1
        g = ha // heads_per_group
        bg = x_ref[:, hd + g * N_B:hd + (g + 1) * N_B]
        cg = x_ref[:, hd + (G_B + g) * N_B:hd + (G_B + g + 1) * N_B]
        cg_r = cg.astype(BF16).astype(F32)
        cb = jnp.sum(cg_r * bg.astype(BF16).astype(F32), axis=-1, keepdims=True)
        x = x_ref[:, j * LANES:(j + 1) * LANES]
        xdt = x * jnp.where(lo, dt[:, ha:ha + 1], dt[:, hb:hb + 1])
        h_prev = h0_ref[ha:hb + 1].reshape(2 * P_B, N_B)
        c8 = jnp.where(first, jnp.broadcast_to(cg, (8, N_B)), 0.0).astype(BF16)
        y_off = _dot_nt(c8, h_prev.astype(BF16))[0:1] * jnp.where(lo, e_a[:, ha:ha + 1], e_a[:, hb:hb + 1])
        x8 = jnp.where(first, jnp.broadcast_to(xdt, (8, LANES)), 0.0)
        b8 = jnp.where(first, jnp.broadcast_to(bg, (8, N_B)), 0.0)
        st = _dot_tn(x8, b8, HIGHEST)
        h_new = h_prev * jnp.where(row_lo, e_a[:, ha:ha + 1], e_a[:, hb:hb + 1]) + st
        h_ref[ha:hb + 1] = h_new.reshape(2, P_B, N_B)
        ys.append(cb * xdt + y_off + x * jnp.where(lo, dskip[:, ha:ha + 1], dskip[:, hb:hb + 1]))
    y = jnp.concatenate(ys, axis=1) * _silu(z_ref[...])
    r = lax.rsqrt(jnp.mean(y * y, axis=-1, keepdims=True) + EPS)
    o_ref[...] = ((y * r) * nw_ref[...]).astype(o_ref.dtype)


def sample_ssd(xc, us, z, h0, dt_bias, a_log, d_skip, norm_w):
    batch, n_heads = h0.shape[0], h0.shape[1]
    hd = n_heads * P_B
    cdim = xc.shape[2]
    rowspec = lambda w: pl.BlockSpec((None, 1, w), lambda b: (b, 0, 0))
    vec = pl.BlockSpec((1, n_heads), lambda b: (0, 0))
    st = pl.BlockSpec((None, n_heads, P_B, N_B), lambda b: (b, 0, 0, 0))
    return pl.pallas_call(
        functools.partial(_sample_ssd_body, hd=hd, n_heads=n_heads),
        grid=(batch,),
        in_specs=[rowspec(cdim), rowspec(LANES), rowspec(hd), st, vec, vec, vec, pl.BlockSpec((1, hd), lambda b: (0, 0))],
        out_specs=[rowspec(hd), st],
        out_shape=[jax.ShapeDtypeStruct((batch, 1, hd), BF16), jax.ShapeDtypeStruct(h0.shape, F32)],
        compiler_params=_params("parallel"),
        name="sample_ssd",
    )(xc, us, z, h0, dt_bias.reshape(1, -1), a_log.reshape(1, -1), d_skip.reshape(1, -1), norm_w.reshape(1, hd))


def _sample_gdn_body(us_ref, alog_ref, dtb_ref, q_ref, k_ref, v_ref, z_ref, s0_ref, nw_ref, o_ref, s_ref,
                     *, a_lane, b_lane):
    b = pl.program_id(0)
    h = pl.program_id(1)
    a_logit = jnp.full((1, LANES), us_ref[b, a_lane + h], F32)
    b_logit = jnp.full((1, LANES), us_ref[b, b_lane + h], F32)
    g = -jnp.exp(jnp.full((1, LANES), alog_ref[h], F32)) * _softplus(a_logit + dtb_ref[h])
    e_g = jnp.exp(g)
    beta = jax.nn.sigmoid(b_logit)
    q = _l2norm(q_ref[...]) * (DK_C ** -0.5)
    k = _l2norm(k_ref[...])
    v = v_ref[...]
    s0 = s0_ref[...]
    row = lax.broadcasted_iota(jnp.int32, (8, LANES), 0)
    lhs = jnp.where(row == 0, k * beta * e_g, jnp.where(row == 1, q * e_g, 0.0))
    ws = _dot(lhs, s0, HIGHEST)
    v_new = v * beta - ws[0:1]
    qk = jnp.sum(q * k, axis=-1, keepdims=True)
    o = ws[1:2] + qk * v_new
    k8 = jnp.where(row == 0, jnp.broadcast_to(k, (8, LANES)), 0.0)
    v8 = jnp.where(row == 0, jnp.broadcast_to(v_new, (8, LANES)), 0.0)
    s_ref[...] = s0 * e_g + _dot_tn(k8, v8, HIGHEST)
    r = lax.rsqrt(jnp.mean(o * o, axis=-1, keepdims=True) + EPS)
    o_ref[...] = ((o * r) * nw_ref[...] * _silu(z_ref[...])).astype(o_ref.dtype)


def sample_gdn(xc, us, z, s0, a_log, dt_bias, norm_w, a_lane, b_lane):
    batch, n_heads = s0.shape[0], s0.shape[1]
    smem = pl.BlockSpec(memory_space=pltpu.SMEM)
    tile = lambda off: pl.BlockSpec((None, None, 1, DK_C), lambda b, h: (b, off + h, 0, 0))
    st = pl.BlockSpec((None, None, DK_C, DK_C), lambda b, h: (b, h, 0, 0))
    return pl.pallas_call(
        functools.partial(_sample_gdn_body, a_lane=a_lane, b_lane=b_lane),
        grid=(batch, n_heads),
        in_specs=[smem, smem, smem, tile(0), tile(n_heads), tile(2 * n_heads), tile(0), st,
                  pl.BlockSpec((1, DK_C), lambda b, h: (0, 0))],
        out_specs=[tile(0), st],
        out_shape=[jax.ShapeDtypeStruct((batch, n_heads, 1, DK_C), BF16), jax.ShapeDtypeStruct(s0.shape, F32)],
        compiler_params=_params("parallel", "parallel"),
        name="sample_gdn",
    )(us, a_log, dt_bias, xc, xc, xc, z, s0, norm_w.reshape(1, DK_C))


def kernel(x_prompt, x_sample, cache_k, cache_v, page_table, state_ssm, state_conv_ssm, state_delta, state_conv_delta, rel_bias_table, norm_w, w_in, lam_q1, lam_k1, lam_q2, lam_k2, subln_w, conv_w_b, conv_bias_b, dt_bias_b, a_log_b, d_skip_b, norm_b_w, conv_w_c, dt_bias_c, a_log_c, norm_c_w, w_branch_a, w_branch_b, w_branch_c, w_out, final_norm_w):
    batch, seq, d = x_prompt.shape
    dec_batch = x_sample.shape[0]
    depth = w_in.shape[0]
    hd = d // 2
    h_a = hd // DV_A
    h_b = hd // P_B
    h_c = hd // DK_C
    conv_b = hd + 2 * G_B * N_B
    conv_c = 3 * hd
    m = batch * seq
    ms = 16

    sizes = (hd, hd, hd, hd, hd, conv_b, h_b, conv_c, hd, h_c, h_c, 3 * d)
    offs = np.concatenate([[0], np.cumsum(sizes)]).tolist()
    o_dt, o_qkvc, o_zc, o_ac, o_bc, o_gate = offs[6], offs[7], offs[8], offs[9], offs[10], offs[11]
    c_xbc = 5 * hd
    c_qkvc = c_xbc + conv_b
    c_zc = c_qkvc + conv_c
    c_gate = c_zc + hd
    a_lane, b_lane = h_b, h_b + h_c
    n_small = h_b + 2 * h_c

    t_attn = min(ATTN_BLOCK, seq)
    bias = bias_tiles(rel_bias_table, t_attn)
    table2 = jnp.tile(rel_bias_table.T, (2, 1))

    xp = x_prompt.reshape(m, d)
    xs = jnp.zeros((ms, d), F32).at[:dec_batch].set(x_sample.reshape(dec_batch, d))
    outs = {k: [] for k in ("kp", "vp", "ks", "vs", "ssmp", "ssms", "cbp", "cbs", "dp", "ds", "ccp", "ccs")}

    for l in range(depth):
        lam_init = 0.8 - 0.6 * math.exp(-0.3 * l)
        lam = jnp.exp(jnp.sum(lam_q1[l] * lam_k1[l])) - jnp.exp(jnp.sum(lam_q2[l] * lam_k2[l])) + lam_init
        scal = jnp.stack([lam, jnp.asarray(1.0 - lam_init, F32)]).astype(F32)
        wl = w_in[l]
        w_main = jnp.concatenate([wl[:, :o_dt], wl[:, o_qkvc:o_ac], wl[:, o_gate:]], axis=1).astype(BF16)
        w_small = jnp.concatenate([wl[:, o_dt:o_qkvc], wl[:, o_ac:o_gate],
                                   jnp.zeros((d, LANES - n_small), F32)], axis=1).astype(BF16)
        wa, wb, wc, wo = (w_branch_a[l].astype(BF16), w_branch_b[l].astype(BF16),
                          w_branch_c[l].astype(BF16), w_out[l].astype(BF16))
        zero_bias_c = jnp.zeros((conv_c,), F32)

        hp = rmsnorm(xp, norm_w[l], BF16)
        u = matmul(hp, w_main)
        us = matmul(hp, w_small)
        u3 = u.reshape(batch, seq, -1)
        outs["kp"].append(u3[:, :, hd:2 * hd].reshape(batch, seq, h_a, DV_A))
        outs["vp"].append(u3[:, :, 2 * hd:3 * hd].reshape(batch, seq, h_a, DV_A))
        outs["cbp"].append(u3[:, seq - (CONV_K - 1):, c_xbc:c_xbc + conv_b])
        outs["ccp"].append(u3[:, seq - (CONV_K - 1):, c_qkvc:c_qkvc + conv_c])

        o_a = prompt_attention(u, bias, subln_w[l], scal, batch, seq, hd, t_attn)
        xc_b = prompt_conv(u, c_xbc, conv_b, conv_w_b[l], conv_bias_b[l], batch, seq)
        o_b, ssm = prompt_ssd(xc_b, us, u, 4, dt_bias_b[l], a_log_b[l], d_skip_b[l], norm_b_w[l], batch, seq, hd)
        xc_c = prompt_conv(u, c_qkvc, conv_c, conv_w_c[l], zero_bias_c, batch, seq)
        o_c, delta = prompt_gdn(xc_c, us, u, c_zc // DK_C, a_log_c[l], dt_bias_c[l], norm_c_w[l],
                                batch, seq, hd, a_lane, b_lane)
        outs["ssmp"].append(ssm)
        outs["dp"].append(delta)
        merged = merge_branches(o_a, o_b, o_c, wa, wb, wc, u, c_gate)
        xp = out_projection(merged, wo, xp)

        hs = rmsnorm(xs, norm_w[l], BF16)
        u_s = matmul(hs, w_main)
        us_s = matmul(hs, w_small)
        ur = u_s[:dec_batch]
        q_s = ur[:, 0:hd].reshape(dec_batch, h_a, DV_A)
        k_s = ur[:, hd:2 * hd].reshape(dec_batch, h_a, DV_A)
        v_s = ur[:, 2 * hd:3 * hd].reshape(dec_batch, h_a, DV_A)
        z_s = ur[:, 3 * hd:4 * hd].reshape(dec_batch, h_a, DV_A)
        outs["ks"].append(k_s.reshape(dec_batch, 1, h_a, DV_A))
        outs["vs"].append(v_s.reshape(dec_batch, 1, h_a, DV_A))
        oa_s = sample_attention(q_s, k_s, v_s, z_s, cache_k, cache_v, l, page_table, table2, subln_w[l], scal)

        xcb_s, cb_state = sample_conv(state_conv_ssm[l], ur[:, c_xbc:c_xbc + conv_b], conv_w_b[l], conv_bias_b[l])
        ob_s, ssm_s = sample_ssd(xcb_s.reshape(dec_batch, 1, conv_b), us_s[:dec_batch].reshape(dec_batch, 1, LANES),
                                 ur[:, 4 * hd:5 * hd].reshape(dec_batch, 1, hd), state_ssm[l],
                                 dt_bias_b[l], a_log_b[l], d_skip_b[l], norm_b_w[l])
        xcc_s, cc_state = sample_conv(state_conv_delta[l], ur[:, c_qkvc:c_qkvc + conv_c], conv_w_c[l], zero_bias_c)
        oc_s, delta_s = sample_gdn(xcc_s.reshape(dec_batch, 3 * h_c, 1, DK_C), us_s,
                                   ur[:, c_zc:c_zc + hd].reshape(dec_batch, h_c, 1, DK_C), state_delta[l],
                                   a_log_c[l], dt_bias_c[l], norm_c_w[l], a_lane, b_lane)
        outs["cbs"].append(cb_state)
        outs["ccs"].append(cc_state)
        outs["ssms"].append(ssm_s)
        outs["ds"].append(delta_s)

        def pad_rows(t):
            return jnp.zeros((ms, hd), BF16).at[:dec_batch].set(t.reshape(dec_batch, hd))

        merged_s = merge_branches(pad_rows(oa_s), pad_rows(ob_s), pad_rows(oc_s), wa, wb, wc, u_s, c_gate)
        xs = out_projection(merged_s, wo, xs)

    y_prompt = rmsnorm(xp, final_norm_w, F32).reshape(batch, seq, d)
    y_sample = rmsnorm(xs, final_norm_w, F32)[:dec_batch].reshape(dec_batch, 1, d)
    st = {k: jnp.stack(v, axis=0) for k, v in outs.items()}
    return (y_prompt, y_sample, st["kp"], st["vp"], st["ks"], st["vs"], st["ssmp"], st["ssms"],
            st["cbp"], st["cbs"], st["dp"], st["ds"], st["ccp"], st["ccs"])
```

```python
import functools
import math

import numpy as np
import jax
import jax.numpy as jnp
from jax import lax
from jax.experimental import pallas as pl
from jax.experimental.pallas import tpu as pltpu

F32 = jnp.float32
BF16 = jnp.bfloat16
HIGHEST = lax.Precision.HIGHEST

DK_A = 64
DV_A = 128
NUM_BUCKETS = 32
MAX_DISTANCE = 128
PAGE_SIZE = 128
P_B = 64
G_B = 4
N_B = 128
DK_C = 128
CONV_K = 4
EPS = 1e-6
NEG = -1e30
LOG2E = math.log2(math.e)

LANES = 128
VMEM_LIMIT = 56 * 1024 * 1024

ATTN_BLOCK = 512
SSD_CHUNK = 128
GDN_CHUNK = 64
GDN_SUPER = 256
GDN_HEADS = 2
DECODE_PAGES = 4


def _params(*sem):
    return pltpu.CompilerParams(dimension_semantics=sem, vmem_limit_bytes=VMEM_LIMIT)


def _pick(n, cands):
    for c in cands:
        if n % c == 0:
            return c
    return n


def _silu(x):
    return x * jax.nn.sigmoid(x)


def _softplus(x):
    return jnp.maximum(x, 0.0) + jnp.log1p(jnp.exp(-jnp.abs(x)))


def _dot(a, b, precision=None):
    return jnp.dot(a, b, preferred_element_type=F32, precision=precision)


def _dot_nt(a, b, precision=None):
    return lax.dot_general(a, b, (((1,), (1,)), ((), ())), preferred_element_type=F32, precision=precision)


def _dot_tn(a, b, precision=None):
    return lax.dot_general(a, b, (((0,), (0,)), ((), ())), preferred_element_type=F32, precision=precision)


def _rmsnorm_body(x_ref, w_ref, o_ref):
    x = x_ref[...]
    r = lax.rsqrt(jnp.mean(x * x, axis=-1, keepdims=True) + EPS)
    o_ref[...] = ((x * r) * w_ref[...]).astype(o_ref.dtype)


def rmsnorm(x, w, out_dtype):
    m, d = x.shape
    bm = _pick(m, (256, 128, 64, 32, 16, 8))
    return pl.pallas_call(
        _rmsnorm_body,
        grid=(m // bm,),
        in_specs=[pl.BlockSpec((bm, d), lambda i: (i, 0)), pl.BlockSpec((1, d), lambda i: (0, 0))],
        out_specs=pl.BlockSpec((bm, d), lambda i: (i, 0)),
        out_shape=jax.ShapeDtypeStruct((m, d), out_dtype),
        compiler_params=_params("parallel"),
        name="rmsnorm",
    )(x, w.reshape(1, d))


def _mm_body(x_ref, w_ref, o_ref):
    o_ref[...] = _dot(x_ref[...], w_ref[...]).astype(o_ref.dtype)


def matmul(x, w, out_dtype=F32):
    m, k = x.shape
    n = w.shape[1]
    bm = _pick(m, (1024, 512, 256, 128, 64, 32, 16))
    bn = _pick(n, (512, 384, 256, 128))
    return pl.pallas_call(
        _mm_body,
        grid=(m // bm, n // bn),
        in_specs=[pl.BlockSpec((bm, k), lambda i, j: (i, 0)), pl.BlockSpec((k, bn), lambda i, j: (0, j))],
        out_specs=pl.BlockSpec((bm, bn), lambda i, j: (i, j)),
        out_shape=jax.ShapeDtypeStruct((m, n), out_dtype),
        compiler_params=_params("parallel", "arbitrary"),
        name="matmul",
    )(x, w)


def _inproj_body(x_ref, w_ref, k_in, v_in, o_ref, k_ref, v_ref, *, tiles):
    del k_in, v_in
    j = pl.program_id(1)
    acc = _dot(x_ref[...], w_ref[...])
    o_ref[...] = acc

    @pl.when((j >= tiles) & (j < 2 * tiles))
    def _():
        k_ref[...] = acc

    @pl.when((j >= 2 * tiles) & (j < 3 * tiles))
    def _():
        v_ref[...] = acc


def input_projection(x, w, k_all, v_all, layer, hd):
    m, k = x.shape
    n = w.shape[1]
    bm = _pick(m, (1024, 512, 256, 128, 64, 32, 16))
    bn = _pick(math.gcd(n, hd), (512, 256, 128))
    tiles = hd // bn
    any_spec = pl.BlockSpec(memory_space=pl.ANY)

    def slab(first):
        return pl.BlockSpec((None, bm, bn), lambda i, j: (layer, i, jnp.clip(j - first, 0, tiles - 1)))

    return pl.pallas_call(
        functools.partial(_inproj_body, tiles=tiles),
        grid=(m // bm, n // bn),
        in_specs=[pl.BlockSpec((bm, k), lambda i, j: (i, 0)), pl.BlockSpec((k, bn), lambda i, j: (0, j)),
                  any_spec, any_spec],
        out_specs=[pl.BlockSpec((bm, bn), lambda i, j: (i, j)), slab(tiles), slab(2 * tiles)],
        out_shape=[jax.ShapeDtypeStruct((m, n), F32), jax.ShapeDtypeStruct(k_all.shape, F32),
                   jax.ShapeDtypeStruct(v_all.shape, F32)],
        input_output_aliases={2: 1, 3: 2},
        compiler_params=_params("arbitrary", "arbitrary"),
        name="inproj",
    )(x, w, k_all, v_all)


def _merge_body(oa_ref, ob_ref, oc_ref, wa_ref, wb_ref, wc_ref, ga_ref, gb_ref, gc_ref, o_ref):
    acc = jax.nn.sigmoid(ga_ref[...]) * _dot(oa_ref[...], wa_ref[...])
    acc = acc + jax.nn.sigmoid(gb_ref[...]) * _dot(ob_ref[...], wb_ref[...])
    acc = acc + jax.nn.sigmoid(gc_ref[...]) * _dot(oc_ref[...], wc_ref[...])
    o_ref[...] = acc.astype(o_ref.dtype)


def merge_branches(o_a, o_b, o_c, w_a, w_b, w_c, u, gate_off):
    m, hd = o_a.shape
    d = w_a.shape[1]
    bm = _pick(m, (512, 256, 128, 64, 32, 16))
    bn = _pick(math.gcd(d, gate_off), (512, 256, 128))
    g0 = gate_off // bn
    nd = d // bn
    row = pl.BlockSpec((bm, hd), lambda i, j: (i, 0))
    wsp = pl.BlockSpec((hd, bn), lambda i, j: (0, j))

    def gate(t):
        return pl.BlockSpec((bm, bn), lambda i, j: (i, g0 + t * nd + j))

    return pl.pallas_call(
        _merge_body,
        grid=(m // bm, nd),
        in_specs=[row, row, row, wsp, wsp, wsp, gate(0), gate(1), gate(2)],
        out_specs=pl.BlockSpec((bm, bn), lambda i, j: (i, j)),
        out_shape=jax.ShapeDtypeStruct((m, d), BF16),
        compiler_params=_params("parallel", "arbitrary"),
        name="merge",
    )(o_a, o_b, o_c, w_a, w_b, w_c, u, u, u)


def _outproj_body(m_ref, w_ref, x_ref, o_ref):
    o_ref[...] = x_ref[...] + _dot(m_ref[...], w_ref[...])


def out_projection(merged, w_out, x):
    m, d = merged.shape
    bm = _pick(m, (1024, 512, 256, 128, 64, 32, 16))
    bn = _pick(d, (512, 256, 128))
    return pl.pallas_call(
        _outproj_body,
        grid=(m // bm, d // bn),
        in_specs=[pl.BlockSpec((bm, d), lambda i, j: (i, 0)), pl.BlockSpec((d, bn), lambda i, j: (0, j)),
                  pl.BlockSpec((bm, bn), lambda i, j: (i, j))],
        out_specs=pl.BlockSpec((bm, bn), lambda i, j: (i, j)),
        out_shape=jax.ShapeDtypeStruct((m, d), F32),
        compiler_params=_params("parallel", "arbitrary"),
        name="outproj",
    )(merged, w_out, x)


def _bucket_changes():
    max_exact = NUM_BUCKETS // 2
    n = np.arange(0, MAX_DISTANCE + 1)
    nf = np.maximum(n, 1).astype(np.float32)
    large = max_exact + (np.log(nf / np.float32(max_exact)) / np.float32(math.log(MAX_DISTANCE / max_exact))
                         * np.float32(NUM_BUCKETS - max_exact)).astype(np.int32)
    bucket = np.where(n < max_exact, n, np.minimum(large, NUM_BUCKETS - 1))
    bucket[MAX_DISTANCE] = NUM_BUCKETS - 1
    changes = [(0, int(bucket[0]))]
    for d in range(1, MAX_DISTANCE + 1):
        if bucket[d] != bucket[d - 1]:
            changes.append((d, int(bucket[d])))
    return changes


def _bias_tiles_body(tbl_ref, o_ref, *, t, n_heads):
    h = pl.program_id(0)
    i = lax.broadcasted_iota(jnp.int32, (t, t), 0)
    j = lax.broadcasted_iota(jnp.int32, (t, t), 1)
    last = tbl_ref[(NUM_BUCKETS - 1) * n_heads + h]
    changes = _bucket_changes()
    for tile, off in ((0, 0), (1, t)):
        d = i - j + off
        val = jnp.full((t, t), (tbl_ref[changes[0][1] * n_heads + h] - last) * LOG2E, F32)
        for ds, b in changes[1:]:
            val = jnp.where(d >= ds, (tbl_ref[b * n_heads + h] - last) * LOG2E, val)
        if off == 0:
            val = jnp.where(j <= i, val, NEG)
        o_ref[tile] = val


def bias_tiles(table, t):
    n_heads = table.shape[1]
    return pl.pallas_call(
        functools.partial(_bias_tiles_body, t=t, n_heads=n_heads),
        grid=(n_heads,),
        in_specs=[pl.BlockSpec(memory_space=pltpu.SMEM)],
        out_specs=pl.BlockSpec((None, 2, t, t), lambda h: (h, 0, 0, 0)),
        out_shape=jax.ShapeDtypeStruct((n_heads, 2, t, t), F32),
        compiler_params=_params("parallel"),
        name="bias_tiles",
    )(table.reshape(-1))


def _attn_body(scal_ref, q_ref, k_ref, v_ref, z_ref, bias_ref, sw_ref, o_ref,
               kb_ref, vb_ref, *, t):
    qi = pl.program_id(2)

    @pl.when(qi == 0)
    def _():
        kb_ref[...] = k_ref[...].astype(BF16)
        vb_ref[...] = v_ref[...].astype(BF16)

    q = q_ref[...] * (DK_A ** -0.5 * LOG2E)
    lane = lax.broadcasted_iota(jnp.int32, q.shape, 1)
    qq = jnp.concatenate([jnp.where(lane < DK_A, q, 0.0), jnp.where(lane >= DK_A, q, 0.0)], axis=0).astype(BF16)

    def step(kc, carry, bias_idx):
        m, l, acc = carry
        start = pl.multiple_of(kc * t, t)
        s = _dot_nt(qq, kb_ref[pl.ds(start, t), :])
        if bias_idx is not None:
            s = (s.reshape(2, t, t) + bias_ref[bias_idx][None]).reshape(2 * t, t)
        m_new = jnp.maximum(m, jnp.max(s, axis=-1, keepdims=True))
        alpha = jnp.exp2(m - m_new)
        p = jnp.exp2(s - m_new)
        l = alpha * l + jnp.sum(p, axis=-1, keepdims=True)
        acc = alpha * acc + _dot(p.astype(BF16), vb_ref[pl.ds(start, t), :])
        return m_new, l, acc

    carry = (jnp.full((2 * t, 1), NEG, F32), jnp.zeros((2 * t, 1), F32), jnp.zeros((2 * t, DV_A), F32))
    far = jnp.maximum(qi - 1, 0)
    carry = lax.fori_loop(0, far, lambda kc, c: step(kc, c, None), carry)
    carry = lax.fori_loop(far, qi, lambda kc, c: step(kc, c, 1), carry)
    _, l, acc = step(qi, carry, 0)

    o = acc / l
    o = o[:t] - scal_ref[0] * o[t:]
    r = lax.rsqrt(jnp.mean(o * o, axis=-1, keepdims=True) + EPS)
    o = (o * r) * sw_ref[...] * scal_ref[1]
    o_ref[...] = (o * _silu(z_ref[...])).astype(o_ref.dtype)


def prompt_attention(u, bias, subln_w, scal, batch, seq, hd, t):
    n_heads = hd // DV_A
    nq = seq // t
    return pl.pallas_call(
        functools.partial(_attn_body, t=t),
        grid=(batch, n_heads, nq),
        in_specs=[
            pl.BlockSpec(memory_space=pltpu.SMEM),
            pl.BlockSpec((t, DV_A), lambda b, h, qi: (b * nq + qi, h)),
            pl.BlockSpec((seq, DV_A), lambda b, h, qi: (b, n_heads + h)),
            pl.BlockSpec((seq, DV_A), lambda b, h, qi: (b, 2 * n_heads + h)),
            pl.BlockSpec((t, DV_A), lambda b, h, qi: (b * nq + qi, 3 * n_heads + h)),
            pl.BlockSpec((None, 2, t, t), lambda b, h, qi: (h, 0, 0, 0)),
            pl.BlockSpec((1, DV_A), lambda b, h, qi: (0, 0)),
        ],
        out_specs=pl.BlockSpec((t, DV_A), lambda b, h, qi: (b * nq + qi, h)),
        out_shape=jax.ShapeDtypeStruct((batch * seq, hd), BF16),
        scratch_shapes=[pltpu.VMEM((seq, DV_A), BF16), pltpu.VMEM((seq, DV_A), BF16)],
        compiler_params=_params("parallel", "parallel", "arbitrary"),
        name="prompt_attn",
    )(scal, u, u, u, u, bias, subln_w.reshape(1, DV_A))


def _conv_body(x_ref, w_ref, b_ref, o_ref):
    x = x_ref[...]
    row = lax.broadcasted_iota(jnp.int32, x.shape, 0)
    acc = x * w_ref[CONV_K - 1:CONV_K, :]
    for k in range(1, CONV_K):
        shifted = jnp.where(row >= k, pltpu.roll(x, k, axis=0), 0.0)
        acc = acc + shifted * w_ref[CONV_K - 1 - k:CONV_K - k, :]
    acc = acc + b_ref[...]
    o_ref[...] = _silu(acc)


def prompt_conv(u, col_off, width, w, bias, batch, seq):
    cw = _pick(math.gcd(col_off, width), (512, 256, 128))
    c0 = col_off // cw
    return pl.pallas_call(
        _conv_body,
        grid=(batch, width // cw),
        in_specs=[pl.BlockSpec((seq, cw), lambda b, c: (b, c0 + c)),
                  pl.BlockSpec((CONV_K, cw), lambda b, c: (0, c)),
                  pl.BlockSpec((1, cw), lambda b, c: (0, c))],
        out_specs=pl.BlockSpec((seq, cw), lambda b, c: (b, c)),
        out_shape=jax.ShapeDtypeStruct((batch * seq, width), F32),
        compiler_params=_params("parallel", "parallel"),
        name="prompt_conv",
    )(u, w, bias.reshape(1, width))


def _sample_conv_body(prev_ref, x_ref, w_ref, b_ref, y_ref, st_ref):
    x = x_ref[...]
    acc = x * w_ref[CONV_K - 1:CONV_K, :]
    for i in range(CONV_K - 1):
        acc = acc + prev_ref[i] * w_ref[i:i + 1, :]
    y_ref[...] = _silu(acc + b_ref[...])
    for i in range(CONV_K - 2):
        st_ref[i] = prev_ref[i + 1]
    st_ref[CONV_K - 2] = x


def sample_conv(prev, x, w, bias):
    rows, c = x.shape
    prev_t = jnp.transpose(prev, (1, 0, 2))
    cw = _pick(c, (1024, 512, 256, 128))
    y, st = pl.pallas_call(
        _sample_conv_body,
        grid=(c // cw,),
        in_specs=[pl.BlockSpec((CONV_K - 1, rows, cw), lambda j: (0, 0, j)),
                  pl.BlockSpec((rows, cw), lambda j: (0, j)),
                  pl.BlockSpec((CONV_K, cw), lambda j: (0, j)),
                  pl.BlockSpec((1, cw), lambda j: (0, j))],
        out_specs=[pl.BlockSpec((rows, cw), lambda j: (0, j)),
                   pl.BlockSpec((CONV_K - 1, rows, cw), lambda j: (0, 0, j))],
        out_shape=[jax.ShapeDtypeStruct((rows, c), F32), jax.ShapeDtypeStruct((CONV_K - 1, rows, c), F32)],
        compiler_params=_params("parallel"),
        name="sample_conv",
    )(prev_t, x, w, bias.reshape(1, c))
    return y, jnp.transpose(st, (1, 0, 2))


def _ssd_body(x_ref, us_ref, z_ref, dtb_ref, alog_ref, dskip_ref, nw_ref, o_ref, h_ref, *, t, hd, n_heads):
    c = pl.program_id(1)

    @pl.when(c == 0)
    def _():
        h_ref[...] = jnp.zeros_like(h_ref)

    heads_per_group = n_heads // G_B
    dt = _softplus(us_ref[:, 0:n_heads] + dtb_ref[...])
    a = dt * (-jnp.exp(alog_ref[...]))
    ri = lax.broadcasted_iota(jnp.int32, (t, t), 0)
    ci = lax.broadcasted_iota(jnp.int32, (t, t), 1)
    lower = ci <= ri
    a_col = _dot(lower.astype(F32), a, HIGHEST)
    a_row = _dot_tn(a, (ri <= ci).astype(F32), HIGHEST)
    a_last = a_col[t - 1:t, :]
    e_col = jnp.exp(a_col)
    e_end = jnp.exp(a_last - a_col)
    e_last = jnp.exp(a_last)
    lane = lax.broadcasted_iota(jnp.int32, (t, LANES), 1)
    lo = lane < P_B
    row_lo = lax.broadcasted_iota(jnp.int32, (2 * P_B, N_B), 0) < P_B
    dskip = dskip_ref[...]

    cb = {}
    ys = []
    for j in range(n_heads // 2):
        ha, hb = 2 * j, 2 * j + 1
        g = ha // heads_per_group
        bg = x_ref[:, hd + g * N_B:hd + (g + 1) * N_B].astype(BF16)
        cg = x_ref[:, hd + (G_B + g) * N_B:hd + (G_B + g + 1) * N_B].astype(BF16)
        if g not in cb:
            cb[g] = _dot_nt(cg, bg)
        x = x_ref[:, j * LANES:(j + 1) * LANES]
        xdt = x * jnp.where(lo, dt[:, ha:ha + 1], dt[:, hb:hb + 1])
        xdt_b = xdt.astype(BF16)
        y = None
        for hh, keep in ((ha, lo), (hb, ~lo)):
            seg = a_col[:, hh:hh + 1] - a_row[hh:hh + 1, :]
            w = (cb[g] * jnp.exp(jnp.where(lower, seg, NEG))).astype(BF16)
            part = _dot(w, jnp.where(keep, xdt_b, jnp.zeros_like(xdt_b)))
            y = part if y is None else y + part
        h_prev = h_ref[ha:hb + 1].reshape(2 * P_B, N_B)
        y = y + _dot_nt(cg, h_prev.astype(BF16)) * jnp.where(lo, e_col[:, ha:ha + 1], e_col[:, hb:hb + 1])
        xdec = (xdt * jnp.where(lo, e_end[:, ha:ha + 1], e_end[:, hb:hb + 1])).astype(BF16)
        st = _dot_tn(xdec, bg)
        h_new = h_prev * jnp.where(row_lo, e_last[:, ha:ha + 1], e_last[:, hb:hb + 1]) + st
        h_ref[ha:hb + 1] = h_new.reshape(2, P_B, N_B)
        y = y + x * jnp.where(lo[:1], dskip[:, ha:ha + 1], dskip[:, hb:hb + 1])
        ys.append(y)
    y = jnp.concatenate(ys, axis=1)
    y = y * _silu(z_ref[...])
    r = lax.rsqrt(jnp.mean(y * y, axis=-1, keepdims=True) + EPS)
    o_ref[...] = ((y * r) * nw_ref[...]).astype(o_ref.dtype)


def prompt_ssd(xc, us, u, z_blk, dt_bias, a_log, d_skip, norm_w, batch, seq, hd):
    n_heads = hd // P_B
    t = min(SSD_CHUNK, seq)
    nc = seq // t
    cdim = xc.shape[1]
    vec = pl.BlockSpec((1, n_heads), lambda b, c: (0, 0))
    return pl.pallas_call(
        functools.partial(_ssd_body, t=t, hd=hd, n_heads=n_heads),
        grid=(batch, nc),
        in_specs=[pl.BlockSpec((t, cdim), lambda b, c: (b * nc + c, 0)),
                  pl.BlockSpec((t, LANES), lambda b, c: (b * nc + c, 0)),
                  pl.BlockSpec((t, hd), lambda b, c: (b * nc + c, z_blk)),
                  vec, vec, vec,
                  pl.BlockSpec((1, hd), lambda b, c: (0, 0))],
        out_specs=[pl.BlockSpec((t, hd), lambda b, c: (b * nc + c, 0)),
                   pl.BlockSpec((None, n_heads, P_B, N_B), lambda b, c: (b, 0, 0, 0))],
        out_shape=[jax.ShapeDtypeStruct((batch * seq, hd), BF16),
                   jax.ShapeDtypeStruct((batch, n_heads, P_B, N_B), F32)],
        compiler_params=_params("parallel", "arbitrary"),
        name="prompt_ssd",
    )(xc, us, u, dt_bias.reshape(1, -1), a_log.reshape(1, -1), d_skip.reshape(1, -1), norm_w.reshape(1, hd))


def _l2norm(x):
    return x * lax.rsqrt(jnp.sum(x * x, axis=-1, keepdims=True) + EPS)


def _split3(x):
    a = x.astype(BF16).astype(F32)
    r = x - a
    b = r.astype(BF16).astype(F32)
    return a, b, (r - b).astype(BF16).astype(F32)


def _gdn_body(alog_ref, dtb_ref, q_ref, k_ref, v_ref, z_ref, us_ref, nw_ref, o_ref, s_ref,
              u_s, wq_s, qk_s, kd_s, gl_s, *, tc, sb, nc, a_lane, b_lane):
    pid = pl.program_id(1)
    cps = sb // tc
    shift = tc.bit_length() - 1
    ri = lax.broadcasted_iota(jnp.int32, (sb, sb), 0)
    ci = lax.broadcasted_iota(jnp.int32, (sb, sb), 1)
    same = lax.shift_right_logical(ri, shift) == lax.shift_right_logical(ci, shift)
    incl = same & (ci <= ri)
    strict = same & (ci < ri)
    eye = jnp.where(ri == ci, 1.0, 0.0)
    tri = jnp.where(incl, 1.0, 0.0).astype(BF16)
    tri_t = jnp.where(same & (ri <= ci), 1.0, 0.0).astype(BF16)
    tri_ones = jnp.concatenate([tri, jnp.where(same, 1.0, 0.0).astype(BF16)], axis=0)
    lane = lax.broadcasted_iota(jnp.int32, (sb, LANES), 1)

    def sum3(x):
        return x[:, 0:1] + x[:, 1:2] + x[:, 2:3]

    def prepare(sidx, hh):
        h = pid * GDN_HEADS + hh
        rows = pl.ds(pl.multiple_of(sidx * sb, sb), sb)
        cols = slice(hh * DK_C, (hh + 1) * DK_C)
        us = us_ref[rows, :]
        a_col = jnp.sum(jnp.where(lane == a_lane + h, us, 0.0), axis=-1, keepdims=True)
        b_col = jnp.sum(jnp.where(lane == b_lane + h, us, 0.0), axis=-1, keepdims=True)
        g = -jnp.exp(jnp.full((1, 1), alog_ref[h], F32)) * _softplus(a_col + dtb_ref[h])
        beta = jax.nn.sigmoid(b_col)
        g1, g2, g3 = _split3(g)
        gm = jnp.where(lane == 0, g1, jnp.where(lane == 1, g2, jnp.where(lane == 2, g3, 0.0))).astype(BF16)
        cum = _dot(tri_ones, gm)
        gc = sum3(cum[:sb])
        g_end = sum3(cum[sb:])
        cum_t = _dot_tn(gm, tri_t)
        gc_row = cum_t[0:1] + cum_t[1:2] + cum_t[2:3]
        decay = jnp.exp(jnp.where(incl, gc - gc_row, NEG))
        q = _l2norm(q_ref[rows, cols]) * (DK_C ** -0.5)
        k = _l2norm(k_ref[rows, cols])
        kb = k * beta
        k_b = k.astype(BF16)
        kq = _dot_nt(jnp.concatenate([kb, q], axis=0).astype(BF16), k_b)
        m = jnp.where(strict, kq[:sb] * decay, 0.0)
        x = eye - m
        p = m.astype(BF16)
        n = 2
        while n <= tc // 2:
            p = _dot(p, p).astype(BF16)
            x = x + _dot(x.astype(BF16), p)
            n *= 2
        a_mat = eye + m
        a_hi = a_mat.astype(BF16)
        a_lo = (a_mat - a_hi.astype(F32)).astype(BF16)
        x_hi = x.astype(BF16)
        x_lo = (x - x_hi.astype(F32)).astype(BF16)
        y = _dot(jnp.concatenate([a_hi, a_lo], axis=0), x_hi)
        resid = eye - (y[:sb] + y[sb:] + _dot(a_hi, x_lo))
        t_inv = (x + _dot(x_hi, resid.astype(BF16))).astype(BF16)
        e_gc = jnp.exp(gc)
        rhs = jnp.concatenate([v_ref[rows, cols] * beta, kb * e_gc], axis=1).astype(BF16)
        uw = _dot(t_inv, rhs)
        u_s[hh, rows, :] = uw[:, :DK_C]
        w = uw[:, DK_C:].astype(BF16)
        qd = (q * e_gc).astype(BF16)
        qk = (kq[sb:] * decay).astype(BF16)
        kd_s[hh, rows, :] = (k * jnp.exp(g_end - gc)).astype(BF16)
        e_end = jnp.exp(g_end)
        for c in range(cps):
            cidx = sidx * cps + c
            r0 = c * tc
            wq_s[hh, cidx, 0:tc, :] = w[r0:r0 + tc]
            wq_s[hh, cidx, tc:2 * tc, :] = qd[r0:r0 + tc]
            qk_s[hh, cidx] = qk[r0:r0 + tc, r0:r0 + tc]
            gl_s[hh, pl.ds(cidx, 1), :] = jnp.broadcast_to(e_end[r0:r0 + 1, :], (1, LANES))

    def prepare_all(sidx, carry):
        for hh in range(GDN_HEADS):
            prepare(sidx, hh)
        return carry

    lax.fori_loop(0, nc // cps, prepare_all, 0)

    def recur(cidx, states):
        rows = pl.ds(pl.multiple_of(cidx * tc, tc), tc)
        new = []
        for hh in range(GDN_HEADS):
            cols = slice(hh * DK_C, (hh + 1) * DK_C)
            s = states[hh]
            s_b = s.astype(BF16)
            ws = _dot(wq_s[hh, cidx], s_b)
            v_b = (u_s[hh, rows, :] - ws[:tc]).astype(BF16)
            o = ws[tc:] + _dot(qk_s[hh, cidx], v_b)
            r = lax.rsqrt(jnp.mean(o * o, axis=-1, keepdims=True) + EPS)
            o_ref[rows, cols] = ((o * r) * nw_ref[...] * _silu(z_ref[rows, cols])).astype(o_ref.dtype)
            new.append(s * gl_s[hh, pl.ds(cidx, 1), :] + _dot_tn(kd_s[hh, rows, :], v_b))
        return tuple(new)

    final = lax.fori_loop(0, nc, recur, tuple(jnp.zeros((DK_C, DK_C), F32) for _ in range(GDN_HEADS)))
    for hh in range(GDN_HEADS):
        s_ref[hh] = final[hh]


def prompt_gdn(xc, us, u, z_col, a_log, dt_bias, norm_w, batch, seq, hd, a_lane, b_lane):
    n_heads = hd // DK_C
    tc = min(GDN_CHUNK, seq)
    sb = min(GDN_SUPER, seq)
    nc = seq // tc
    gw = GDN_HEADS * DK_C
    smem = pl.BlockSpec(memory_space=pltpu.SMEM)
    heads = lambda col: pl.BlockSpec((seq, gw), lambda b, h: (b, col // gw + h))
    return pl.pallas_call(
        functools.partial(_gdn_body, tc=tc, sb=sb, nc=nc, a_lane=a_lane, b_lane=b_lane),
        grid=(batch, n_heads // GDN_HEADS),
        in_specs=[smem, smem, heads(0), heads(hd), heads(2 * hd), heads(z_col),
                  pl.BlockSpec((seq, LANES), lambda b, h: (b, 0)),
                  pl.BlockSpec((1, DK_C), lambda b, h: (0, 0))],
        out_specs=[pl.BlockSpec((seq, gw), lambda b, h: (b, h)),
                   pl.BlockSpec((None, GDN_HEADS, DK_C, DK_C), lambda b, h: (b, h, 0, 0))],
        out_shape=[jax.ShapeDtypeStruct((batch * seq, hd), BF16),
                   jax.ShapeDtypeStruct((batch, n_heads, DK_C, DK_C), F32)],
        scratch_shapes=[pltpu.VMEM((GDN_HEADS, seq, DK_C), F32),
                        pltpu.VMEM((GDN_HEADS, nc, 2 * tc, DK_C), BF16),
                        pltpu.VMEM((GDN_HEADS, nc, tc, tc), BF16),
                        pltpu.VMEM((GDN_HEADS, seq, DK_C), BF16),
                        pltpu.VMEM((GDN_HEADS, nc, LANES), F32)],
        compiler_params=_params("parallel", "parallel"),
        name="prompt_gdn",
    )(a_log, dt_bias, xc, xc, xc, u, us, norm_w.reshape(1, DK_C))


def _sample_attn_body(pt_ref, scal_ref, q_ref, kn_ref, vn_ref, z_ref, tbl_ref, sw_ref, *refs,
                      n_heads, n_steps, pages):
    k_refs, v_refs = refs[:pages], refs[pages:2 * pages]
    o_ref, m_s, l_s, acc_s = refs[2 * pages:]
    p = pl.program_id(1)
    rows = 2 * n_heads
    cols = PAGE_SIZE * n_heads

    @pl.when(p == 0)
    def _():
        m_s[...] = jnp.full_like(m_s, NEG)
        l_s[...] = jnp.zeros_like(l_s)
        acc_s[...] = jnp.zeros_like(acc_s)

    q = q_ref[...] * (DK_A ** -0.5)
    lane = lax.broadcasted_iota(jnp.int32, q.shape, 1)
    qq = jnp.concatenate([jnp.where(lane < DK_A, q, 0.0), jnp.where(lane >= DK_A, q, 0.0)], axis=0).astype(BF16)
    r_i = lax.broadcasted_iota(jnp.int32, (rows, cols), 0)
    c_i = lax.broadcasted_iota(jnp.int32, (rows, cols), 1)
    own = (c_i % n_heads) == (r_i % n_heads)
    last = tbl_ref[:, NUM_BUCKETS - 1:NUM_BUCKETS]
    changes = _bucket_changes()
    delta0 = tbl_ref[:, changes[0][1]:changes[0][1] + 1] - last

    def add_bias(s):
        d = PAGE_SIZE - c_i // n_heads
        val = jnp.broadcast_to(delta0, (rows, cols))
        for ds, b in changes[1:]:
            val = jnp.where(d >= ds, tbl_ref[:, b:b + 1] - last, val)
        return s + val

    ss = [_dot_nt(qq, k_refs[i][...].reshape(cols, DV_A).astype(BF16)) for i in range(pages)]
    ss[-1] = lax.cond(p == n_steps - 1, add_bias, lambda s: s, ss[-1])
    ss = [jnp.where(own, s, NEG) for s in ss]
    m_new = m_s[...]
    for s in ss:
        m_new = jnp.maximum(m_new, jnp.max(s, axis=-1, keepdims=True))
    alpha = jnp.exp(m_s[...] - m_new)
    l_new = alpha * l_s[...]
    acc = alpha * acc_s[...]
    for i, s in enumerate(ss):
        pr = jnp.where(own, jnp.exp(s - m_new), 0.0)
        l_new = l_new + jnp.sum(pr, axis=-1, keepdims=True)
        acc = acc + _dot(pr.astype(BF16), v_refs[i][...].reshape(cols, DV_A).astype(BF16))
    m_s[...] = m_new
    l_s[...] = l_new
    acc_s[...] = acc

    @pl.when(p == n_steps - 1)
    def _():
        kn = jnp.concatenate([kn_ref[...], kn_ref[...]], axis=0).astype(BF16).astype(F32)
        vn = jnp.concatenate([vn_ref[...], vn_ref[...]], axis=0).astype(BF16).astype(F32)
        s_new = jnp.sum(qq.astype(F32) * kn, axis=-1, keepdims=True) + delta0
        m_fin = jnp.maximum(m_new, s_new)
        a2 = jnp.exp(m_new - m_fin)
        p_new = jnp.exp(s_new - m_fin)
        l = a2 * l_new + p_new
        o = (a2 * acc + p_new.astype(BF16).astype(F32) * vn) / l
        o = o[:n_heads] - scal_ref[0] * o[n_heads:]
        r = lax.rsqrt(jnp.mean(o * o, axis=-1, keepdims=True) + EPS)
        o = (o * r) * sw_ref[...] * scal_ref[1]
        o_ref[...] = (o * _silu(z_ref[...])).astype(o_ref.dtype)


def sample_attention(q, k_new, v_new, z, cache_k, cache_v, layer, page_table, table2, subln_w, scal):
    batch, n_heads, _ = q.shape
    n_pages = page_table.shape[1]
    pages = _pick(n_pages, (DECODE_PAGES, 2, 1))
    n_steps = n_pages // pages
    per_seq = pl.BlockSpec((None, n_heads, DV_A), lambda b, p, pt: (b, 0, 0))

    def page(i):
        return pl.BlockSpec((None, None, PAGE_SIZE, n_heads, DV_A),
                            lambda b, p, pt: (layer, pt[b * n_pages + p * pages + i], 0, 0, 0))

    grid_spec = pltpu.PrefetchScalarGridSpec(
        num_scalar_prefetch=1,
        grid=(batch, n_steps),
        in_specs=[pl.BlockSpec(memory_space=pltpu.SMEM), per_seq, per_seq, per_seq, per_seq,
                  pl.BlockSpec((2 * n_heads, NUM_BUCKETS), lambda b, p, pt: (0, 0)),
                  pl.BlockSpec((1, DV_A), lambda b, p, pt: (0, 0))]
                 + [page(i) for i in range(pages)] + [page(i) for i in range(pages)],
        out_specs=per_seq,
        scratch_shapes=[pltpu.VMEM((2 * n_heads, 1), F32), pltpu.VMEM((2 * n_heads, 1), F32),
                        pltpu.VMEM((2 * n_heads, DV_A), F32)],
    )
    return pl.pallas_call(
        functools.partial(_sample_attn_body, n_heads=n_heads, n_steps=n_steps, pages=pages),
        grid_spec=grid_spec,
        out_shape=jax.ShapeDtypeStruct((batch, n_heads, DV_A), BF16),
        compiler_params=_params("parallel", "arbitrary"),
        name="sample_attn",
    )(page_table.reshape(-1), scal, q, k_new, v_new, z, table2, subln_w.reshape(1, DV_A),
      *([cache_k] * pages), *([cache_v] * pages))


def _sample_ssd_body(x_ref, us_ref, z_ref, h0_ref, dtb_ref, alog_ref, dskip_ref, nw_ref, o_ref, h_ref, *, hd, n_heads):
    heads_per_group = n_heads // G_B
    dt = _softplus(us_ref[:, 0:n_heads] + dtb_ref[...])
    e_a = jnp.exp(dt * (-jnp.exp(alog_ref[...])))
    lane = lax.broadcasted_iota(jnp.int32, (1, LANES), 1)
    lo = lane < P_B
    row_lo = lax.broadcasted_iota(jnp.int32, (2 * P_B, N_B), 0) < P_B
    first = lax.broadcasted_iota(jnp.int32, (8, LANES), 0) == 0
    dskip = dskip_ref[...]
    ys = []
    for j in range(n_heads // 2):
        ha, hb = 2 * j, 2 * j + 1
        g = ha // heads_per_group
        bg = x_ref[:, hd + g * N_B:hd + (g + 1) * N_B]
        cg = x_ref[:, hd + (G_B + g) * N_B:hd + (G_B + g + 1) * N_B]
        cg_r = cg.astype(BF16).astype(F32)
        cb = jnp.sum(cg_r * bg.astype(BF16).astype(F32), axis=-1, keepdims=True)
        x = x_ref[:, j * LANES:(j + 1) * LANES]
        xdt = x * jnp.where(lo, dt[:, ha:ha + 1], dt[:, hb:hb + 1])
        h_prev = h0_ref[ha:hb + 1].reshape(2 * P_B, N_B)
        c8 = jnp.where(first, jnp.broadcast_to(cg, (8, N_B)), 0.0).astype(BF16)
        y_off = _dot_nt(c8, h_prev.astype(BF16))[0:1] * jnp.where(lo, e_a[:, ha:ha + 1], e_a[:, hb:hb + 1])
        x8 = jnp.where(first, jnp.broadcast_to(xdt, (8, LANES)), 0.0)
        b8 = jnp.where(first, jnp.broadcast_to(bg, (8, N_B)), 0.0)
        st = _dot_tn(x8, b8, HIGHEST)
        h_new = h_prev * jnp.where(row_lo, e_a[:, ha:ha + 1], e_a[:, hb:hb + 1]) + st
        h_ref[ha:hb + 1] = h_new.reshape(2, P_B, N_B)
        ys.append(cb * xdt + y_off + x * jnp.where(lo, dskip[:, ha:ha + 1], dskip[:, hb:hb + 1]))
    y = jnp.concatenate(ys, axis=1) * _silu(z_ref[...])
    r = lax.rsqrt(jnp.mean(y * y, axis=-1, keepdims=True) + EPS)
    o_ref[...] = ((y * r) * nw_ref[...]).astype(o_ref.dtype)


def sample_ssd(xc, us, z, h0, dt_bias, a_log, d_skip, norm_w):
    batch, n_heads = h0.shape[0], h0.shape[1]
    hd = n_heads * P_B
    cdim = xc.shape[2]
    rowspec = lambda w: pl.BlockSpec((None, 1, w), lambda b: (b, 0, 0))
    vec = pl.BlockSpec((1, n_heads), lambda b: (0, 0))
    st = pl.BlockSpec((None, n_heads, P_B, N_B), lambda b: (b, 0, 0, 0))
    return pl.pallas_call(
        functools.partial(_sample_ssd_body, hd=hd, n_heads=n_heads),
        grid=(batch,),
        in_specs=[rowspec(cdim), rowspec(LANES), rowspec(hd), st, vec, vec, vec, pl.BlockSpec((1, hd), lambda b: (0, 0))],
        out_specs=[rowspec(hd), st],
        out_shape=[jax.ShapeDtypeStruct((batch, 1, hd), BF16), jax.ShapeDtypeStruct(h0.shape, F32)],
        compiler_params=_params("parallel"),
        name="sample_ssd",
    )(xc, us, z, h0, dt_bias.reshape(1, -1), a_log.reshape(1, -1), d_skip.reshape(1, -1), norm_w.reshape(1, hd))


def _sample_gdn_body(us_ref, alog_ref, dtb_ref, q_ref, k_ref, v_ref, z_ref, s0_ref, nw_ref, o_ref, s_ref,
                     *, a_lane, b_lane):
    b = pl.program_id(0)
    h = pl.program_id(1)
    a_logit = jnp.full((1, LANES), us_ref[b, a_lane + h], F32)
    b_logit = jnp.full((1, LANES), us_ref[b, b_lane + h], F32)
    g = -jnp.exp(jnp.full((1, LANES), alog_ref[h], F32)) * _softplus(a_logit + dtb_ref[h])
    e_g = jnp.exp(g)
    beta = jax.nn.sigmoid(b_logit)
    q = _l2norm(q_ref[...]) * (DK_C ** -0.5)
    k = _l2norm(k_ref[...])
    v = v_ref[...]
    s0 = s0_ref[...]
    row = lax.broadcasted_iota(jnp.int32, (8, LANES), 0)
    lhs = jnp.where(row == 0, k * beta * e_g, jnp.where(row == 1, q * e_g, 0.0))
    ws = _dot(lhs, s0, HIGHEST)
    v_new = v * beta - ws[0:1]
    qk = jnp.sum(q * k, axis=-1, keepdims=True)
    o = ws[1:2] + qk * v_new
    k8 = jnp.where(row == 0, jnp.broadcast_to(k, (8, LANES)), 0.0)
    v8 = jnp.where(row == 0, jnp.broadcast_to(v_new, (8, LANES)), 0.0)
    s_ref[...] = s0 * e_g + _dot_tn(k8, v8, HIGHEST)
    r = lax.rsqrt(jnp.mean(o * o, axis=-1, keepdims=True) + EPS)
    o_ref[...] = ((o * r) * nw_ref[...] * _silu(z_ref[...])).astype(o_ref.dtype)


def sample_gdn(xc, us, z, s0, a_log, dt_bias, norm_w, a_lane, b_lane):
    batch, n_heads = s0.shape[0], s0.shape[1]
    smem = pl.BlockSpec(memory_space=pltpu.SMEM)
    tile = lambda off: pl.BlockSpec((None, None, 1, DK_C), lambda b, h: (b, off + h, 0, 0))
    st = pl.BlockSpec((None, None, DK_C, DK_C), lambda b, h: (b, h, 0, 0))
    return pl.pallas_call(
        functools.partial(_sample_gdn_body, a_lane=a_lane, b_lane=b_lane),
        grid=(batch, n_heads),
        in_specs=[smem, smem, smem, tile(0), tile(n_heads), tile(2 * n_heads), tile(0), st,
                  pl.BlockSpec((1, DK_C), lambda b, h: (0, 0))],
        out_specs=[tile(0), st],
        out_shape=[jax.ShapeDtypeStruct((batch, n_heads, 1, DK_C), BF16), jax.ShapeDtypeStruct(s0.shape, F32)],
        compiler_params=_params("parallel", "parallel"),
        name="sample_gdn",
    )(us, a_log, dt_bias, xc, xc, xc, z, s0, norm_w.reshape(1, DK_C))


def kernel(x_prompt, x_sample, cache_k, cache_v, page_table, state_ssm, state_conv_ssm, state_delta, state_conv_delta, rel_bias_table, norm_w, w_in, lam_q1, lam_k1, lam_q2, lam_k2, subln_w, conv_w_b, conv_bias_b, dt_bias_b, a_log_b, d_skip_b, norm_b_w, conv_w_c, dt_bias_c, a_log_c, norm_c_w, w_branch_a, w_branch_b, w_branch_c, w_out, final_norm_w):
    batch, seq, d = x_prompt.shape
    dec_batch = x_sample.shape[0]
    depth = w_in.shape[0]
    hd = d // 2
    h_a = hd // DV_A
    h_b = hd // P_B
    h_c = hd // DK_C
    conv_b = hd + 2 * G_B * N_B
    conv_c = 3 * hd
    m = batch * seq
    ms = 16

    sizes = (hd, hd, hd, hd, hd, conv_b, h_b, conv_c, hd, h_c, h_c, 3 * d)
    offs = np.concatenate([[0], np.cumsum(sizes)]).tolist()
    o_dt, o_qkvc, o_ac, o_gate = offs[6], offs[7], offs[9], offs[11]
    c_xbc = 5 * hd
    c_qkvc = c_xbc + conv_b
    c_zc = c_qkvc + conv_c
    c_gate = c_zc + hd
    a_lane, b_lane = h_b, h_b + h_c
    n_small = h_b + 2 * h_c

    t_attn = min(ATTN_BLOCK, seq)
    bias = bias_tiles(rel_bias_table, t_attn)
    table2 = jnp.tile(rel_bias_table.T, (2, 1))

    xp = x_prompt.reshape(m, d)
    xs = jnp.zeros((ms, d), F32).at[:dec_batch].set(x_sample.reshape(dec_batch, d))
    k_all = jnp.zeros((depth, m, hd), F32)
    v_all = jnp.zeros((depth, m, hd), F32)
    outs = {k: [] for k in ("ks", "vs", "ssmp", "ssms", "cbp", "cbs", "dp", "ds", "ccp", "ccs")}

    for l in range(depth):
        lam_init = 0.8 - 0.6 * math.exp(-0.3 * l)
        lam = jnp.exp(jnp.sum(lam_q1[l] * lam_k1[l])) - jnp.exp(jnp.sum(lam_q2[l] * lam_k2[l])) + lam_init
        scal = jnp.stack([lam, jnp.asarray(1.0 - lam_init, F32)]).astype(F32)
        wl = w_in[l]
        w_main = jnp.concatenate([wl[:, :o_dt], wl[:, o_qkvc:o_ac], wl[:, o_gate:]], axis=1).astype(BF16)
        w_small = jnp.concatenate([wl[:, o_dt:o_qkvc], wl[:, o_ac:o_gate],
                                   jnp.zeros((d, LANES - n_small), F32)], axis=1).astype(BF16)
        wa, wb, wc, wo = (w_branch_a[l].astype(BF16), w_branch_b[l].astype(BF16),
                          w_branch_c[l].astype(BF16), w_out[l].astype(BF16))
        zero_bias_c = jnp.zeros((conv_c,), F32)

        hp = rmsnorm(xp, norm_w[l], BF16)
        u, k_all, v_all = input_projection(hp, w_main, k_all, v_all, l, hd)
        us = matmul(hp, w_small)
        u3 = u.reshape(batch, seq, -1)
        outs["cbp"].append(u3[:, seq - (CONV_K - 1):, c_xbc:c_xbc + conv_b])
        outs["ccp"].append(u3[:, seq - (CONV_K - 1):, c_qkvc:c_qkvc + conv_c])

        o_a = prompt_attention(u, bias, subln_w[l], scal, batch, seq, hd, t_attn)
        xc_b = prompt_conv(u, c_xbc, conv_b, conv_w_b[l], conv_bias_b[l], batch, seq)
        o_b, ssm = prompt_ssd(xc_b, us, u, 4, dt_bias_b[l], a_log_b[l], d_skip_b[l], norm_b_w[l], batch, seq, hd)
        xc_c = prompt_conv(u, c_qkvc, conv_c, conv_w_c[l], zero_bias_c, batch, seq)
        o_c, delta = prompt_gdn(xc_c, us, u, c_zc, a_log_c[l], dt_bias_c[l], norm_c_w[l],
                                batch, seq, hd, a_lane, b_lane)
        outs["ssmp"].append(ssm)
        outs["dp"].append(delta)
        merged = merge_branches(o_a, o_b, o_c, wa, wb, wc, u, c_gate)
        xp = out_projection(merged, wo, xp)

        hs = rmsnorm(xs, norm_w[l], BF16)
        u_s = matmul(hs, w_main)
        us_s = matmul(hs, w_small)
        ur = u_s[:dec_batch]
        q_s = ur[:, 0:hd].reshape(dec_batch, h_a, DV_A)
        k_s = ur[:, hd:2 * hd].reshape(dec_batch, h_a, DV_A)
        v_s = ur[:, 2 * hd:3 * hd].reshape(dec_batch, h_a, DV_A)
        z_s = ur[:, 3 * hd:4 * hd].reshape(dec_batch, h_a, DV_A)
        outs["ks"].append(k_s.reshape(dec_batch, 1, h_a, DV_A))
        outs["vs"].append(v_s.reshape(dec_batch, 1, h_a, DV_A))
        oa_s = sample_attention(q_s, k_s, v_s, z_s, cache_k, cache_v, l, page_table, table2, subln_w[l], scal)

        xcb_s, cb_state = sample_conv(state_conv_ssm[l], ur[:, c_xbc:c_xbc + conv_b], conv_w_b[l], conv_bias_b[l])
        ob_s, ssm_s = sample_ssd(xcb_s.reshape(dec_batch, 1, conv_b), us_s[:dec_batch].reshape(dec_batch, 1, LANES),
                                 ur[:, 4 * hd:5 * hd].reshape(dec_batch, 1, hd), state_ssm[l],
                                 dt_bias_b[l], a_log_b[l], d_skip_b[l], norm_b_w[l])
        xcc_s, cc_state = sample_conv(state_conv_delta[l], ur[:, c_qkvc:c_qkvc + conv_c], conv_w_c[l], zero_bias_c)
        oc_s, delta_s = sample_gdn(xcc_s.reshape(dec_batch, 3 * h_c, 1, DK_C), us_s,
                                   ur[:, c_zc:c_zc + hd].reshape(dec_batch, h_c, 1, DK_C), state_delta[l],
                                   a_log_c[l], dt_bias_c[l], norm_c_w[l], a_lane, b_lane)
        outs["cbs"].append(cb_state)
        outs["ccs"].append(cc_state)
        outs["ssms"].append(ssm_s)
        outs["ds"].append(delta_s)

        def pad_rows(t):
            return jnp.zeros((ms, hd), BF16).at[:dec_batch].set(t.reshape(dec_batch, hd))

        merged_s = merge_branches(pad_rows(oa_s), pad_rows(ob_s), pad_rows(oc_s), wa, wb, wc, u_s, c_gate)
        xs = out_projection(merged_s, wo, xs)

    y_prompt = rmsnorm(xp, final_norm_w, F32).reshape(batch, seq, d)
    y_sample = rmsnorm(xs, final_norm_w, F32)[:dec_batch].reshape(dec_batch, 1, d)
    st = {k: jnp.stack(v, axis=0) for k, v in outs.items()}
    k_prompt = k_all.reshape(depth, batch, seq, h_a, DV_A)
    v_prompt = v_all.reshape(depth, batch, seq, h_a, DV_A)
    return (y_prompt, y_sample, k_prompt, v_prompt, st["ks"], st["vs"], st["ssmp"], st["ssms"],
            st["cbp"], st["cbs"], st["dp"], st["ds"], st["ccp"], st["ccs"])
```

```python
import functools
import math

import numpy as np
import jax
import jax.numpy as jnp
from jax import lax
from jax.experimental import pallas as pl
from jax.experimental.pallas import tpu as pltpu

F32 = jnp.float32
BF16 = jnp.bfloat16
HIGHEST = lax.Precision.HIGHEST

DK_A = 64
DV_A = 128
NUM_BUCKETS = 32
MAX_DISTANCE = 128
PAGE_SIZE = 128
P_B = 64
G_B = 4
N_B = 128
DK_C = 128
CONV_K = 4
EPS = 1e-6
NEG = -1e30
LOG2E = math.log2(math.e)

LANES = 128
VMEM_LIMIT = 56 * 1024 * 1024

ATTN_BLOCK = 512
SSD_CHUNK = 128
GDN_CHUNK = 64
GDN_SUPER = 256
GDN_HEADS = 2
DECODE_PAGES = 4


def _params(*sem):
    return pltpu.CompilerParams(dimension_semantics=sem, vmem_limit_bytes=VMEM_LIMIT)


def _pick(n, cands):
    for c in cands:
        if n % c == 0:
            return c
    return n


def _silu(x):
    return x * jax.nn.sigmoid(x)


def _softplus(x):
    return jnp.maximum(x, 0.0) + jnp.log1p(jnp.exp(-jnp.abs(x)))


def _dot(a, b, precision=None):
    return jnp.dot(a, b, preferred_element_type=F32, precision=precision)


def _dot_nt(a, b, precision=None):
    return lax.dot_general(a, b, (((1,), (1,)), ((), ())), preferred_element_type=F32, precision=precision)


def _dot_tn(a, b, precision=None):
    return lax.dot_general(a, b, (((0,), (0,)), ((), ())), preferred_element_type=F32, precision=precision)


def _rmsnorm_body(x_ref, w_ref, o_ref):
    x = x_ref[...]
    r = lax.rsqrt(jnp.mean(x * x, axis=-1, keepdims=True) + EPS)
    o_ref[...] = ((x * r) * w_ref[...]).astype(o_ref.dtype)


def rmsnorm(x, w, out_dtype):
    m, d = x.shape
    bm = _pick(m, (256, 128, 64, 32, 16, 8))
    return pl.pallas_call(
        _rmsnorm_body,
        grid=(m // bm,),
        in_specs=[pl.BlockSpec((bm, d), lambda i: (i, 0)), pl.BlockSpec((1, d), lambda i: (0, 0))],
        out_specs=pl.BlockSpec((bm, d), lambda i: (i, 0)),
        out_shape=jax.ShapeDtypeStruct((m, d), out_dtype),
        compiler_params=_params("parallel"),
        name="rmsnorm",
    )(x, w.reshape(1, d))


CAST_ROWS = 256


def _stage_weights(w_ref, e_ref, dst_ref, shift):
    k, width = dst_ref.shape
    rows_per = min(CAST_ROWS, k)
    groups = width // LANES

    def body(r, carry):
        rows = pl.ds(pl.multiple_of(r * rows_per, rows_per), rows_per)
        if shift == 0:
            dst_ref[rows, :] = w_ref[rows, :].astype(BF16)
        else:
            lane = lax.broadcasted_iota(jnp.int32, (rows_per, LANES), 1)
            pieces = [w_ref[rows, g * LANES:(g + 1) * LANES] for g in range(groups)] + [e_ref[rows, :]]
            rolled = [pltpu.roll(p, LANES - shift, axis=1) for p in pieces]
            out = [jnp.where(lane < LANES - shift, rolled[g], rolled[g + 1]) for g in range(groups)]
            dst_ref[rows, :] = jnp.concatenate(out, axis=1).astype(BF16)
        return carry

    lax.fori_loop(0, k // rows_per, body, 0)


def _inproj_body(x_ref, w_ref, e_ref, *refs, tiles, regions, with_kv):
    if with_kv:
        _, _, o_ref, k_ref, v_ref, wb_ref = refs
    else:
        o_ref, wb_ref = refs
    j = pl.program_id(0)
    i = pl.program_id(1)

    @pl.when(i == 0)
    def _():
        for lo, hi, shift in regions:
            @pl.when((j >= lo) & (j < hi))
            def _():
                _stage_weights(w_ref, e_ref, wb_ref, shift)

    acc = _dot(x_ref[...], wb_ref[...])
    o_ref[...] = acc.astype(o_ref.dtype)
    if with_kv:
        @pl.when((j >= tiles) & (j < 2 * tiles))
        def _():
            k_ref[...] = acc

        @pl.when((j >= 2 * tiles) & (j < 3 * tiles))
        def _():
            v_ref[...] = acc


def input_projection(x, w_in, layer, hd, n_out, regions, out_dtype, k_all=None, v_all=None):
    m, k = x.shape
    bm = _pick(m, (1024, 512, 256, 128, 64, 32, 16))
    bn = _pick(math.gcd(n_out, hd), (512, 256, 128))
    tiles = hd // bn
    nj, ni = n_out // bn, m // bm
    with_kv = k_all is not None
    any_spec = pl.BlockSpec(memory_space=pl.ANY)

    def slab(first):
        def index(j, i):
            row = jnp.where(j < first, 0, jnp.where(j >= first + tiles, ni - 1, i))
            return (layer, row, jnp.clip(j - first, 0, tiles - 1))
        return pl.BlockSpec((None, bm, bn), index)

    in_specs = [pl.BlockSpec((bm, k), lambda j, i: (i, 0)),
                pl.BlockSpec((None, k, bn), lambda j, i: (layer, 0, j)),
                pl.BlockSpec((None, k, LANES), lambda j, i: (layer, 0, (j + 1) * (bn // LANES)))]
    out_specs = [pl.BlockSpec((bm, bn), lambda j, i: (i, j))]
    out_shape = [jax.ShapeDtypeStruct((m, n_out), out_dtype)]
    args = [x, w_in, w_in]
    aliases = {}
    if with_kv:
        in_specs += [any_spec, any_spec]
        out_specs += [slab(tiles), slab(2 * tiles)]
        out_shape += [jax.ShapeDtypeStruct(k_all.shape, F32), jax.ShapeDtypeStruct(v_all.shape, F32)]
        args += [k_all, v_all]
        aliases = {3: 1, 4: 2}
    out = pl.pallas_call(
        functools.partial(_inproj_body, tiles=tiles, regions=regions, with_kv=with_kv),
        grid=(nj, ni),
        in_specs=in_specs,
        out_specs=out_specs,
        out_shape=out_shape,
        scratch_shapes=[pltpu.VMEM((k, bn), BF16)],
        input_output_aliases=aliases,
        compiler_params=_params("arbitrary", "arbitrary"),
        name="inproj",
    )(*args)
    return out if with_kv else out[0]


def _small_logits_body(x_ref, wa_ref, wb_ref, oa_ref, ob_ref):
    x = x_ref[...]
    oa_ref[...] = _dot(x, wa_ref[...].astype(BF16))
    ob_ref[...] = _dot(x, wb_ref[...].astype(BF16))


def small_logits(x, w_in, layer, blk_a, blk_b):
    m, k = x.shape
    bm = _pick(m, (1024, 512, 256, 128, 64, 32, 16))
    wspec = lambda blk: pl.BlockSpec((None, k, LANES), lambda i: (layer, 0, blk))
    ospec = pl.BlockSpec((bm, LANES), lambda i: (i, 0))
    return pl.pallas_call(
        _small_logits_body,
        grid=(m // bm,),
        in_specs=[pl.BlockSpec((bm, k), lambda i: (i, 0)), wspec(blk_a), wspec(blk_b)],
        out_specs=[ospec, ospec],
        out_shape=[jax.ShapeDtypeStruct((m, LANES), F32)] * 2,
        compiler_params=_params("parallel"),
        name="small_logits",
    )(x, w_in, w_in)


def _merge_body(oa_ref, ob_ref, oc_ref, wa_ref, wb_ref, wc_ref, ga_ref, gb_ref, gc_ref, o_ref, sa, sb, sc):
    @pl.when(pl.program_id(1) == 0)
    def _():
        sa[...] = wa_ref[...].astype(BF16)
        sb[...] = wb_ref[...].astype(BF16)
        sc[...] = wc_ref[...].astype(BF16)

    acc = jax.nn.sigmoid(ga_ref[...].astype(F32)) * _dot(oa_ref[...], sa[...])
    acc = acc + jax.nn.sigmoid(gb_ref[...].astype(F32)) * _dot(ob_ref[...], sb[...])
    acc = acc + jax.nn.sigmoid(gc_ref[...].astype(F32)) * _dot(oc_ref[...], sc[...])
    o_ref[...] = acc.astype(o_ref.dtype)


def merge_branches(o_a, o_b, o_c, w_a, w_b, w_c, layer, u, gate_off):
    m, hd = o_a.shape
    d = w_a.shape[2]
    bm = _pick(m, (512, 256, 128, 64, 32, 16))
    bn = _pick(math.gcd(d, gate_off), (512, 256, 128))
    g0 = gate_off // bn
    nd = d // bn
    row = pl.BlockSpec((bm, hd), lambda j, i: (i, 0))
    wsp = pl.BlockSpec((None, hd, bn), lambda j, i: (layer, 0, j))

    def gate(t):
        return pl.BlockSpec((bm, bn), lambda j, i: (i, g0 + t * nd + j))

    return pl.pallas_call(
        _merge_body,
        grid=(nd, m // bm),
        in_specs=[row, row, row, wsp, wsp, wsp, gate(0), gate(1), gate(2)],
        out_specs=pl.BlockSpec((bm, bn), lambda j, i: (i, j)),
        out_shape=jax.ShapeDtypeStruct((m, d), BF16),
        scratch_shapes=[pltpu.VMEM((hd, bn), BF16)] * 3,
        compiler_params=_params("parallel", "arbitrary"),
        name="merge",
    )(o_a, o_b, o_c, w_a, w_b, w_c, u, u, u)


def _outproj_body(m_ref, w_ref, x_ref, o_ref, ws):
    @pl.when(pl.program_id(1) == 0)
    def _():
        ws[...] = w_ref[...].astype(BF16)

    o_ref[...] = x_ref[...] + _dot(m_ref[...], ws[...])


def out_projection(merged, w_out, layer, x):
    m, d = merged.shape
    bm = _pick(m, (1024, 512, 256, 128, 64, 32, 16))
    bn = _pick(d, (512, 256, 128))
    return pl.pallas_call(
        _outproj_body,
        grid=(d // bn, m // bm),
        in_specs=[pl.BlockSpec((bm, d), lambda j, i: (i, 0)), pl.BlockSpec((None, d, bn), lambda j, i: (layer, 0, j)),
                  pl.BlockSpec((bm, bn), lambda j, i: (i, j))],
        out_specs=pl.BlockSpec((bm, bn), lambda j, i: (i, j)),
        out_shape=jax.ShapeDtypeStruct((m, d), F32),
        scratch_shapes=[pltpu.VMEM((d, bn), BF16)],
        compiler_params=_params("parallel", "arbitrary"),
        name="outproj",
    )(merged, w_out, x)


def _bucket_changes():
    max_exact = NUM_BUCKETS // 2
    n = np.arange(0, MAX_DISTANCE + 1)
    nf = np.maximum(n, 1).astype(np.float32)
    large = max_exact + (np.log(nf / np.float32(max_exact)) / np.float32(math.log(MAX_DISTANCE / max_exact))
                         * np.float32(NUM_BUCKETS - max_exact)).astype(np.int32)
    bucket = np.where(n < max_exact, n, np.minimum(large, NUM_BUCKETS - 1))
    bucket[MAX_DISTANCE] = NUM_BUCKETS - 1
    changes = [(0, int(bucket[0]))]
    for d in range(1, MAX_DISTANCE + 1):
        if bucket[d] != bucket[d - 1]:
            changes.append((d, int(bucket[d])))
    return changes


def _bias_tiles_body(tbl_ref, o_ref, *, t, n_heads):
    h = pl.program_id(0)
    i = lax.broadcasted_iota(jnp.int32, (t, t), 0)
    j = lax.broadcasted_iota(jnp.int32, (t, t), 1)
    last = tbl_ref[(NUM_BUCKETS - 1) * n_heads + h]
    changes = _bucket_changes()
    for tile, off in ((0, 0), (1, t)):
        d = i - j + off
        val = jnp.full((t, t), (tbl_ref[changes[0][1] * n_heads + h] - last) * LOG2E, F32)
        for ds, b in changes[1:]:
            val = jnp.where(d >= ds, (tbl_ref[b * n_heads + h] - last) * LOG2E, val)
        if off == 0:
            val = jnp.where(j <= i, val, NEG)
        o_ref[tile] = val


def bias_tiles(table, t):
    n_heads = table.shape[1]
    return pl.pallas_call(
        functools.partial(_bias_tiles_body, t=t, n_heads=n_heads),
        grid=(n_heads,),
        in_specs=[pl.BlockSpec(memory_space=pltpu.SMEM)],
        out_specs=pl.BlockSpec((None, 2, t, t), lambda h: (h, 0, 0, 0)),
        out_shape=jax.ShapeDtypeStruct((n_heads, 2, t, t), F32),
        compiler_params=_params("parallel"),
        name="bias_tiles",
    )(table.reshape(-1))


def _attn_body(scal_ref, q_ref, kb_ref, vb_ref, z_ref, bias_ref, sw_ref, o_ref, *, t):
    qi = pl.program_id(2)
    q = q_ref[...].astype(F32) * (DK_A ** -0.5 * LOG2E)
    lane = lax.broadcasted_iota(jnp.int32, q.shape, 1)
    qq = jnp.concatenate([jnp.where(lane < DK_A, q, 0.0), jnp.where(lane >= DK_A, q, 0.0)], axis=0).astype(BF16)

    def step(kc, carry, bias_idx):
        m, l, acc = carry
        start = pl.multiple_of(kc * t, t)
        s = _dot_nt(qq, kb_ref[pl.ds(start, t), :])
        if bias_idx is not None:
            s = (s.reshape(2, t, t) + bias_ref[bias_idx][None]).reshape(2 * t, t)
        m_new = jnp.maximum(m, jnp.max(s, axis=-1, keepdims=True))
        alpha = jnp.exp2(m - m_new)
        p = jnp.exp2(s - m_new)
        l = alpha * l + jnp.sum(p, axis=-1, keepdims=True)
        acc = alpha * acc + _dot(p.astype(BF16), vb_ref[pl.ds(start, t), :])
        return m_new, l, acc

    carry = (jnp.full((2 * t, 1), NEG, F32), jnp.zeros((2 * t, 1), F32), jnp.zeros((2 * t, DV_A), F32))
    far = jnp.maximum(qi - 1, 0)
    carry = lax.fori_loop(0, far, lambda kc, c: step(kc, c, None), carry)
    carry = lax.fori_loop(far, qi, lambda kc, c: step(kc, c, 1), carry)
    _, l, acc = step(qi, carry, 0)

    o = acc / l
    o = o[:t] - scal_ref[0] * o[t:]
    r = lax.rsqrt(jnp.mean(o * o, axis=-1, keepdims=True) + EPS)
    o = (o * r) * sw_ref[...] * scal_ref[1]
    o_ref[...] = (o * _silu(z_ref[...].astype(F32))).astype(o_ref.dtype)


def prompt_attention(u, bias, subln_w, scal, batch, seq, hd, t):
    n_heads = hd // DV_A
    nq = seq // t
    return pl.pallas_call(
        functools.partial(_attn_body, t=t),
        grid=(batch, n_heads, nq),
        in_specs=[
            pl.BlockSpec(memory_space=pltpu.SMEM),
            pl.BlockSpec((t, DV_A), lambda b, h, qi: (b * nq + qi, h)),
            pl.BlockSpec((seq, DV_A), lambda b, h, qi: (b, n_heads + h)),
            pl.BlockSpec((seq, DV_A), lambda b, h, qi: (b, 2 * n_heads + h)),
            pl.BlockSpec((t, DV_A), lambda b, h, qi: (b * nq + qi, 3 * n_heads + h)),
            pl.BlockSpec((None, 2, t, t), lambda b, h, qi: (h, 0, 0, 0)),
            pl.BlockSpec((1, DV_A), lambda b, h, qi: (0, 0)),
        ],
        out_specs=pl.BlockSpec((t, DV_A), lambda b, h, qi: (b * nq + qi, h)),
        out_shape=jax.ShapeDtypeStruct((batch * seq, hd), BF16),
        compiler_params=_params("parallel", "parallel", "arbitrary"),
        name="prompt_attn",
    )(scal, u, u, u, u, bias, subln_w.reshape(1, DV_A))


def _conv_body(x_ref, w_ref, b_ref, o_ref):
    x = x_ref[...].astype(F32)
    row = lax.broadcasted_iota(jnp.int32, x.shape, 0)
    acc = x * w_ref[CONV_K - 1:CONV_K, :]
    for k in range(1, CONV_K):
        shifted = jnp.where(row >= k, pltpu.roll(x, k, axis=0), 0.0)
        acc = acc + shifted * w_ref[CONV_K - 1 - k:CONV_K - k, :]
    acc = acc + b_ref[...]
    o_ref[...] = _silu(acc)


def prompt_conv(u, col_off, width, w, bias, batch, seq):
    cw = _pick(math.gcd(col_off, width), (512, 256, 128))
    c0 = col_off // cw
    return pl.pallas_call(
        _conv_body,
        grid=(batch, width // cw),
        in_specs=[pl.BlockSpec((seq, cw), lambda b, c: (b, c0 + c)),
                  pl.BlockSpec((CONV_K, cw), lambda b, c: (0, c)),
                  pl.BlockSpec((1, cw), lambda b, c: (0, c))],
        out_specs=pl.BlockSpec((seq, cw), lambda b, c: (b, c)),
        out_shape=jax.ShapeDtypeStruct((batch * seq, width), F32),
        compiler_params=_params("parallel", "parallel"),
        name="prompt_conv",
    )(u, w, bias.reshape(1, width))


def _sample_conv_body(prev_ref, x_ref, w_ref, b_ref, y_ref, st_ref):
    x = x_ref[...]
    acc = x * w_ref[CONV_K - 1:CONV_K, :]
    for i in range(CONV_K - 1):
        acc = acc + prev_ref[i] * w_ref[i:i + 1, :]
    y_ref[...] = _silu(acc + b_ref[...])
    for i in range(CONV_K - 2):
        st_ref[i] = prev_ref[i + 1]
    st_ref[CONV_K - 2] = x


def sample_conv(prev, x, w, bias):
    rows, c = x.shape
    prev_t = jnp.transpose(prev, (1, 0, 2))
    cw = _pick(c, (1024, 512, 256, 128))
    y, st = pl.pallas_call(
        _sample_conv_body,
        grid=(c // cw,),
        in_specs=[pl.BlockSpec((CONV_K - 1, rows, cw), lambda j: (0, 0, j)),
                  pl.BlockSpec((rows, cw), lambda j: (0, j)),
                  pl.BlockSpec((CONV_K, cw), lambda j: (0, j)),
                  pl.BlockSpec((1, cw), lambda j: (0, j))],
        out_specs=[pl.BlockSpec((rows, cw), lambda j: (0, j)),
                   pl.BlockSpec((CONV_K - 1, rows, cw), lambda j: (0, 0, j))],
        out_shape=[jax.ShapeDtypeStruct((rows, c), F32), jax.ShapeDtypeStruct((CONV_K - 1, rows, c), F32)],
        compiler_params=_params("parallel"),
        name="sample_conv",
    )(prev_t, x, w, bias.reshape(1, c))
    return y, jnp.transpose(st, (1, 0, 2))


def _ssd_body(x_ref, us_ref, z_ref, dtb_ref, alog_ref, dskip_ref, nw_ref, o_ref, h_ref, *, t, hd, n_heads):
    c = pl.program_id(1)

    @pl.when(c == 0)
    def _():
        h_ref[...] = jnp.zeros_like(h_ref)

    heads_per_group = n_heads // G_B
    dt = _softplus(us_ref[:, 0:n_heads] + dtb_ref[...])
    a = dt * (-jnp.exp(alog_ref[...]))
    ri = lax.broadcasted_iota(jnp.int32, (t, t), 0)
    ci = lax.broadcasted_iota(jnp.int32, (t, t), 1)
    lower = ci <= ri
    a_col = _dot(lower.astype(F32), a, HIGHEST)
    a_row = _dot_tn(a, (ri <= ci).astype(F32), HIGHEST)
    a_last = a_col[t - 1:t, :]
    e_col = jnp.exp(a_col)
    e_end = jnp.exp(a_last - a_col)
    e_last = jnp.exp(a_last)
    lane = lax.broadcasted_iota(jnp.int32, (t, LANES), 1)
    lo = lane < P_B
    row_lo = lax.broadcasted_iota(jnp.int32, (2 * P_B, N_B), 0) < P_B
    dskip = dskip_ref[...]

    cb = {}
    ys = []
    for j in range(n_heads // 2):
        ha, hb = 2 * j, 2 * j + 1
        g = ha // heads_per_group
        bg = x_ref[:, hd + g * N_B:hd + (g + 1) * N_B].astype(BF16)
        cg = x_ref[:, hd + (G_B + g) * N_B:hd + (G_B + g + 1) * N_B].astype(BF16)
        if g not in cb:
            cb[g] = _dot_nt(cg, bg)
        x = x_ref[:, j * LANES:(j + 1) * LANES]
        xdt = x * jnp.where(lo, dt[:, ha:ha + 1], dt[:, hb:hb + 1])
        xdt_b = xdt.astype(BF16)
        y = None
        for hh, keep in ((ha, lo), (hb, ~lo)):
            seg = a_col[:, hh:hh + 1] - a_row[hh:hh + 1, :]
            w = (cb[g] * jnp.exp(jnp.where(lower, seg, NEG))).astype(BF16)
            part = _dot(w, jnp.where(keep, xdt_b, jnp.zeros_like(xdt_b)))
            y = part if y is None else y + part
        h_prev = h_ref[ha:hb + 1].reshape(2 * P_B, N_B)
        y = y + _dot_nt(cg, h_prev.astype(BF16)) * jnp.where(lo, e_col[:, ha:ha + 1], e_col[:, hb:hb + 1])
        xdec = (xdt * jnp.where(lo, e_end[:, ha:ha + 1], e_end[:, hb:hb + 1])).astype(BF16)
        st = _dot_tn(xdec, bg)
        h_new = h_prev * jnp.where(row_lo, e_last[:, ha:ha + 1], e_last[:, hb:hb + 1]) + st
        h_ref[ha:hb + 1] = h_new.reshape(2, P_B, N_B)
        y = y + x * jnp.where(lo[:1], dskip[:, ha:ha + 1], dskip[:, hb:hb + 1])
        ys.append(y)
    y = jnp.concatenate(ys, axis=1)
    y = y * _silu(z_ref[...].astype(F32))
    r = lax.rsqrt(jnp.mean(y * y, axis=-1, keepdims=True) + EPS)
    o_ref[...] = ((y * r) * nw_ref[...]).astype(o_ref.dtype)


def prompt_ssd(xc, us, u, z_blk, dt_bias, a_log, d_skip, norm_w, batch, seq, hd):
    n_heads = hd // P_B
    t = min(SSD_CHUNK, seq)
    nc = seq // t
    cdim = xc.shape[1]
    vec = pl.BlockSpec((1, n_heads), lambda b, c: (0, 0))
    return pl.pallas_call(
        functools.partial(_ssd_body, t=t, hd=hd, n_heads=n_heads),
        grid=(batch, nc),
        in_specs=[pl.BlockSpec((t, cdim), lambda b, c: (b * nc + c, 0)),
                  pl.BlockSpec((t, LANES), lambda b, c: (b * nc + c, 0)),
                  pl.BlockSpec((t, hd), lambda b, c: (b * nc + c, z_blk)),
                  vec, vec, vec,
                  pl.BlockSpec((1, hd), lambda b, c: (0, 0))],
        out_specs=[pl.BlockSpec((t, hd), lambda b, c: (b * nc + c, 0)),
                   pl.BlockSpec((None, n_heads, P_B, N_B), lambda b, c: (b, 0, 0, 0))],
        out_shape=[jax.ShapeDtypeStruct((batch * seq, hd), BF16),
                   jax.ShapeDtypeStruct((batch, n_heads, P_B, N_B), F32)],
        compiler_params=_params("parallel", "arbitrary"),
        name="prompt_ssd",
    )(xc, us, u, dt_bias.reshape(1, -1), a_log.reshape(1, -1), d_skip.reshape(1, -1), norm_w.reshape(1, hd))


def _l2norm(x):
    return x * lax.rsqrt(jnp.sum(x * x, axis=-1, keepdims=True) + EPS)


def _split3(x):
    a = x.astype(BF16).astype(F32)
    r = x - a
    b = r.astype(BF16).astype(F32)
    return a, b, (r - b).astype(BF16).astype(F32)


def _gdn_body(alog_ref, dtb_ref, q_ref, k_ref, v_ref, z_ref, us_ref, nw_ref, o_ref, s_ref,
              u_s, wq_s, qk_s, kd_s, gl_s, *, tc, sb, nc, a_lane, b_lane):
    pid = pl.program_id(1)
    cps = sb // tc
    shift = tc.bit_length() - 1
    ri = lax.broadcasted_iota(jnp.int32, (sb, sb), 0)
    ci = lax.broadcasted_iota(jnp.int32, (sb, sb), 1)
    same = lax.shift_right_logical(ri, shift) == lax.shift_right_logical(ci, shift)
    incl = same & (ci <= ri)
    strict = same & (ci < ri)
    eye = jnp.where(ri == ci, 1.0, 0.0)
    tri = jnp.where(incl, 1.0, 0.0).astype(BF16)
    tri_t = jnp.where(same & (ri <= ci), 1.0, 0.0).astype(BF16)
    tri_ones = jnp.concatenate([tri, jnp.where(same, 1.0, 0.0).astype(BF16)], axis=0)
    lane = lax.broadcasted_iota(jnp.int32, (sb, LANES), 1)

    def sum3(x):
        return x[:, 0:1] + x[:, 1:2] + x[:, 2:3]

    def prepare(sidx, hh):
        h = pid * GDN_HEADS + hh
        rows = pl.ds(pl.multiple_of(sidx * sb, sb), sb)
        cols = slice(hh * DK_C, (hh + 1) * DK_C)
        us = us_ref[rows, :]
        a_col = jnp.sum(jnp.where(lane == a_lane + h, us, 0.0), axis=-1, keepdims=True)
        b_col = jnp.sum(jnp.where(lane == b_lane + h, us, 0.0), axis=-1, keepdims=True)
        g = -jnp.exp(jnp.full((1, 1), alog_ref[h], F32)) * _softplus(a_col + dtb_ref[h])
        beta = jax.nn.sigmoid(b_col)
        g1, g2, g3 = _split3(g)
        gm = jnp.where(lane == 0, g1, jnp.where(lane == 1, g2, jnp.where(lane == 2, g3, 0.0))).astype(BF16)
        cum = _dot(tri_ones, gm)
        gc = sum3(cum[:sb])
        g_end = sum3(cum[sb:])
        cum_t = _dot_tn(gm, tri_t)
        gc_row = cum_t[0:1] + cum_t[1:2] + cum_t[2:3]
        decay = jnp.exp(jnp.where(incl, gc - gc_row, NEG))
        q = _l2norm(q_ref[rows, cols]) * (DK_C ** -0.5)
        k = _l2norm(k_ref[rows, cols])
        kb = k * beta
        k_b = k.astype(BF16)
        kq = _dot_nt(jnp.concatenate([kb, q], axis=0).astype(BF16), k_b)
        m = jnp.where(strict, kq[:sb] * decay, 0.0)
        x = eye - m
        p = m.astype(BF16)
        n = 2
        while n <= tc // 2:
            p = _dot(p, p).astype(BF16)
            x = x + _dot(x.astype(BF16), p)
            n *= 2
        a_mat = eye + m
        a_hi = a_mat.astype(BF16)
        a_lo = (a_mat - a_hi.astype(F32)).astype(BF16)
        x_hi = x.astype(BF16)
        x_lo = (x - x_hi.astype(F32)).astype(BF16)
        y = _dot(jnp.concatenate([a_hi, a_lo], axis=0), x_hi)
        resid = eye - (y[:sb] + y[sb:] + _dot(a_hi, x_lo))
        t_inv = (x + _dot(x_hi, resid.astype(BF16))).astype(BF16)
        e_gc = jnp.exp(gc)
        rhs = jnp.concatenate([v_ref[rows, cols] * beta, kb * e_gc], axis=1).astype(BF16)
        uw = _dot(t_inv, rhs)
        u_s[hh, rows, :] = uw[:, :DK_C]
        w = uw[:, DK_C:].astype(BF16)
        qd = (q * e_gc).astype(BF16)
        qk = (kq[sb:] * decay).astype(BF16)
        kd_s[hh, rows, :] = (k * jnp.exp(g_end - gc)).astype(BF16)
        e_end = jnp.exp(g_end)
        for c in range(cps):
            cidx = sidx * cps + c
            r0 = c * tc
            wq_s[hh, cidx, 0:tc, :] = w[r0:r0 + tc]
            wq_s[hh, cidx, tc:2 * tc, :] = qd[r0:r0 + tc]
            qk_s[hh, cidx] = qk[r0:r0 + tc, r0:r0 + tc]
            gl_s[hh, pl.ds(cidx, 1), :] = jnp.broadcast_to(e_end[r0:r0 + 1, :], (1, LANES))

    def prepare_all(sidx, carry):
        for hh in range(GDN_HEADS):
            prepare(sidx, hh)
        return carry

    lax.fori_loop(0, nc // cps, prepare_all, 0)

    def recur(cidx, states):
        rows = pl.ds(pl.multiple_of(cidx * tc, tc), tc)
        new = []
        for hh in range(GDN_HEADS):
            cols = slice(hh * DK_C, (hh + 1) * DK_C)
            s = states[hh]
            s_b = s.astype(BF16)
            ws = _dot(wq_s[hh, cidx], s_b)
            v_b = (u_s[hh, rows, :] - ws[:tc]).astype(BF16)
            o = ws[tc:] + _dot(qk_s[hh, cidx], v_b)
            r = lax.rsqrt(jnp.mean(o * o, axis=-1, keepdims=True) + EPS)
            o_ref[rows, cols] = ((o * r) * nw_ref[...] * _silu(z_ref[rows, cols].astype(F32))).astype(o_ref.dtype)
            new.append(s * gl_s[hh, pl.ds(cidx, 1), :] + _dot_tn(kd_s[hh, rows, :], v_b))
        return tuple(new)

    final = lax.fori_loop(0, nc, recur, tuple(jnp.zeros((DK_C, DK_C), F32) for _ in range(GDN_HEADS)))
    for hh in range(GDN_HEADS):
        s_ref[hh] = final[hh]


def prompt_gdn(xc, us, u, z_col, a_log, dt_bias, norm_w, batch, seq, hd, a_lane, b_lane):
    n_heads = hd // DK_C
    tc = min(GDN_CHUNK, seq)
    sb = min(GDN_SUPER, seq)
    nc = seq // tc
    gw = GDN_HEADS * DK_C
    smem = pl.BlockSpec(memory_space=pltpu.SMEM)
    heads = lambda col: pl.BlockSpec((seq, gw), lambda b, h: (b, col // gw + h))
    return pl.pallas_call(
        functools.partial(_gdn_body, tc=tc, sb=sb, nc=nc, a_lane=a_lane, b_lane=b_lane),
        grid=(batch, n_heads // GDN_HEADS),
        in_specs=[smem, smem, heads(0), heads(hd), heads(2 * hd), heads(z_col),
                  pl.BlockSpec((seq, LANES), lambda b, h: (b, 0)),
                  pl.BlockSpec((1, DK_C), lambda b, h: (0, 0))],
        out_specs=[pl.BlockSpec((seq, gw), lambda b, h: (b, h)),
                   pl.BlockSpec((None, GDN_HEADS, DK_C, DK_C), lambda b, h: (b, h, 0, 0))],
        out_shape=[jax.ShapeDtypeStruct((batch * seq, hd), BF16),
                   jax.ShapeDtypeStruct((batch, n_heads, DK_C, DK_C), F32)],
        scratch_shapes=[pltpu.VMEM((GDN_HEADS, seq, DK_C), F32),
                        pltpu.VMEM((GDN_HEADS, nc, 2 * tc, DK_C), BF16),
                        pltpu.VMEM((GDN_HEADS, nc, tc, tc), BF16),
                        pltpu.VMEM((GDN_HEADS, seq, DK_C), BF16),
                        pltpu.VMEM((GDN_HEADS, nc, LANES), F32)],
        compiler_params=_params("parallel", "parallel"),
        name="prompt_gdn",
    )(a_log, dt_bias, xc, xc, xc, u, us, norm_w.reshape(1, DK_C))


def _sample_attn_body(pt_ref, scal_ref, q_ref, kn_ref, vn_ref, z_ref, tbl_ref, sw_ref, *refs,
                      n_heads, n_steps, pages):
    k_refs, v_refs = refs[:pages], refs[pages:2 * pages]
    o_ref, m_s, l_s, acc_s = refs[2 * pages:]
    p = pl.program_id(1)
    rows = 2 * n_heads
    cols = PAGE_SIZE * n_heads

    @pl.when(p == 0)
    def _():
        m_s[...] = jnp.full_like(m_s, NEG)
        l_s[...] = jnp.zeros_like(l_s)
        acc_s[...] = jnp.zeros_like(acc_s)

    q = q_ref[...] * (DK_A ** -0.5)
    lane = lax.broadcasted_iota(jnp.int32, q.shape, 1)
    qq = jnp.concatenate([jnp.where(lane < DK_A, q, 0.0), jnp.where(lane >= DK_A, q, 0.0)], axis=0).astype(BF16)
    r_i = lax.broadcasted_iota(jnp.int32, (rows, cols), 0)
    c_i = lax.broadcasted_iota(jnp.int32, (rows, cols), 1)
    own = (c_i % n_heads) == (r_i % n_heads)
    last = tbl_ref[:, NUM_BUCKETS - 1:NUM_BUCKETS]
    changes = _bucket_changes()
    delta0 = tbl_ref[:, changes[0][1]:changes[0][1] + 1] - last

    def add_bias(s):
        d = PAGE_SIZE - c_i // n_heads
        val = jnp.broadcast_to(delta0, (rows, cols))
        for ds, b in changes[1:]:
            val = jnp.where(d >= ds, tbl_ref[:, b:b + 1] - last, val)
        return s + val

    ss = [_dot_nt(qq, k_refs[i][...].reshape(cols, DV_A).astype(BF16)) for i in range(pages)]
    ss[-1] = lax.cond(p == n_steps - 1, add_bias, lambda s: s, ss[-1])
    ss = [jnp.where(own, s, NEG) for s in ss]
    m_new = m_s[...]
    for s in ss:
        m_new = jnp.maximum(m_new, jnp.max(s, axis=-1, keepdims=True))
    alpha = jnp.exp(m_s[...] - m_new)
    l_new = alpha * l_s[...]
    acc = alpha * acc_s[...]
    for i, s in enumerate(ss):
        pr = jnp.where(own, jnp.exp(s - m_new), 0.0)
        l_new = l_new + jnp.sum(pr, axis=-1, keepdims=True)
        acc = acc + _dot(pr.astype(BF16), v_refs[i][...].reshape(cols, DV_A).astype(BF16))
    m_s[...] = m_new
    l_s[...] = l_new
    acc_s[...] = acc

    @pl.when(p == n_steps - 1)
    def _():
        kn = jnp.concatenate([kn_ref[...], kn_ref[...]], axis=0).astype(BF16).astype(F32)
        vn = jnp.concatenate([vn_ref[...], vn_ref[...]], axis=0).astype(BF16).astype(F32)
        s_new = jnp.sum(qq.astype(F32) * kn, axis=-1, keepdims=True) + delta0
        m_fin = jnp.maximum(m_new, s_new)
        a2 = jnp.exp(m_new - m_fin)
        p_new = jnp.exp(s_new - m_fin)
        l = a2 * l_new + p_new
        o = (a2 * acc + p_new.astype(BF16).astype(F32) * vn) / l
        o = o[:n_heads] - scal_ref[0] * o[n_heads:]
        r = lax.rsqrt(jnp.mean(o * o, axis=-1, keepdims=True) + EPS)
        o = (o * r) * sw_ref[...] * scal_ref[1]
        o_ref[...] = (o * _silu(z_ref[...])).astype(o_ref.dtype)


def sample_attention(q, k_new, v_new, z, cache_k, cache_v, layer, page_table, table2, subln_w, scal):
    batch, n_heads, _ = q.shape
    n_pages = page_table.shape[1]
    pages = _pick(n_pages, (DECODE_PAGES, 2, 1))
    n_steps = n_pages // pages
    per_seq = pl.BlockSpec((None, n_heads, DV_A), lambda b, p, pt: (b, 0, 0))

    def page(i):
        return pl.BlockSpec((None, None, PAGE_SIZE, n_heads, DV_A),
                            lambda b, p, pt: (layer, pt[b * n_pages + p * pages + i], 0, 0, 0))

    grid_spec = pltpu.PrefetchScalarGridSpec(
        num_scalar_prefetch=1,
        grid=(batch, n_steps),
        in_specs=[pl.BlockSpec(memory_space=pltpu.SMEM), per_seq, per_seq, per_seq, per_seq,
                  pl.BlockSpec((2 * n_heads, NUM_BUCKETS), lambda b, p, pt: (0, 0)),
                  pl.BlockSpec((1, DV_A), lambda b, p, pt: (0, 0))]
                 + [page(i) for i in range(pages)] + [page(i) for i in range(pages)],
        out_specs=per_seq,
        scratch_shapes=[pltpu.VMEM((2 * n_heads, 1), F32), pltpu.VMEM((2 * n_heads, 1), F32),
                        pltpu.VMEM((2 * n_heads, DV_A), F32)],
    )
    return pl.pallas_call(
        functools.partial(_sample_attn_body, n_heads=n_heads, n_steps=n_steps, pages=pages),
        grid_spec=grid_spec,
        out_shape=jax.ShapeDtypeStruct((batch, n_heads, DV_A), BF16),
        compiler_params=_params("parallel", "arbitrary"),
        name="sample_attn",
    )(page_table.reshape(-1), scal, q, k_new, v_new, z, table2, subln_w.reshape(1, DV_A),
      *([cache_k] * pages), *([cache_v] * pages))


def _sample_ssd_body(x_ref, us_ref, z_ref, h0_ref, dtb_ref, alog_ref, dskip_ref, nw_ref, o_ref, h_ref, *, hd, n_heads):
    heads_per_group = n_heads // G_B
    dt = _softplus(us_ref[:, 0:n_heads] + dtb_ref[...])
    e_a = jnp.exp(dt * (-jnp.exp(alog_ref[...])))
    lane = lax.broadcasted_iota(jnp.int32, (1, LANES), 1)
    lo = lane < P_B
    row_lo = lax.broadcasted_iota(jnp.int32, (2 * P_B, N_B), 0) < P_B
    first = lax.broadcasted_iota(jnp.int32, (8, LANES), 0) == 0
    dskip = dskip_ref[...]
    ys = []
    for j in range(n_heads // 2):
        ha, hb = 2 * j, 2 * j + 1
        g = ha // heads_per_group
        bg = x_ref[:, hd + g * N_B:hd + (g + 1) * N_B]
        cg = x_ref[:, hd + (G_B + g) * N_B:hd + (G_B + g + 1) * N_B]
        cg_r = cg.astype(BF16).astype(F32)
        cb = jnp.sum(cg_r * bg.astype(BF16).astype(F32), axis=-1, keepdims=True)
        x = x_ref[:, j * LANES:(j + 1) * LANES]
        xdt = x * jnp.where(lo, dt[:, ha:ha + 1], dt[:, hb:hb + 1])
        h_prev = h0_ref[ha:hb + 1].reshape(2 * P_B, N_B)
        c8 = jnp.where(first, jnp.broadcast_to(cg, (8, N_B)), 0.0).astype(BF16)
        y_off = _dot_nt(c8, h_prev.astype(BF16))[0:1] * jnp.where(lo, e_a[:, ha:ha + 1], e_a[:, hb:hb + 1])
        x8 = jnp.where(first, jnp.broadcast_to(xdt, (8, LANES)), 0.0)
        b8 = jnp.where(first, jnp.broadcast_to(bg, (8, N_B)), 0.0)
        st = _dot_tn(x8, b8, HIGHEST)
        h_new = h_prev * jnp.where(row_lo, e_a[:, ha:ha + 1], e_a[:, hb:hb + 1]) + st
        h_ref[ha:hb + 1] = h_new.reshape(2, P_B, N_B)
        ys.append(cb * xdt + y_off + x * jnp.where(lo, dskip[:, ha:ha + 1], dskip[:, hb:hb + 1]))
    y = jnp.concatenate(ys, axis=1) * _silu(z_ref[...])
    r = lax.rsqrt(jnp.mean(y * y, axis=-1, keepdims=True) + EPS)
    o_ref[...] = ((y * r) * nw_ref[...]).astype(o_ref.dtype)


def sample_ssd(xc, us, z, h0, dt_bias, a_log, d_skip, norm_w):
    batch, n_heads = h0.shape[0], h0.shape[1]
    hd = n_heads * P_B
    cdim = xc.shape[2]
    rowspec = lambda w: pl.BlockSpec((None, 1, w), lambda b: (b, 0, 0))
    vec = pl.BlockSpec((1, n_heads), lambda b: (0, 0))
    st = pl.BlockSpec((None, n_heads, P_B, N_B), lambda b: (b, 0, 0, 0))
    return pl.pallas_call(
        functools.partial(_sample_ssd_body, hd=hd, n_heads=n_heads),
        grid=(batch,),
        in_specs=[rowspec(cdim), rowspec(LANES), rowspec(hd), st, vec, vec, vec, pl.BlockSpec((1, hd), lambda b: (0, 0))],
        out_specs=[rowspec(hd), st],
        out_shape=[jax.ShapeDtypeStruct((batch, 1, hd), BF16), jax.ShapeDtypeStruct(h0.shape, F32)],
        compiler_params=_params("parallel"),
        name="sample_ssd",
    )(xc, us, z, h0, dt_bias.reshape(1, -1), a_log.reshape(1, -1), d_skip.reshape(1, -1), norm_w.reshape(1, hd))


def _sample_gdn_body(us_ref, alog_ref, dtb_ref, q_ref, k_ref, v_ref, z_ref, s0_ref, nw_ref, o_ref, s_ref,
                     *, a_lane, b_lane):
    b = pl.program_id(0)
    h = pl.program_id(1)
    a_logit = jnp.full((1, LANES), us_ref[b, a_lane + h], F32)
    b_logit = jnp.full((1, LANES), us_ref[b, b_lane + h], F32)
    g = -jnp.exp(jnp.full((1, LANES), alog_ref[h], F32)) * _softplus(a_logit + dtb_ref[h])
    e_g = jnp.exp(g)
    beta = jax.nn.sigmoid(b_logit)
    q = _l2norm(q_ref[...]) * (DK_C ** -0.5)
    k = _l2norm(k_ref[...])
    v = v_ref[...]
    s0 = s0_ref[...]
    row = lax.broadcasted_iota(jnp.int32, (8, LANES), 0)
    lhs = jnp.where(row == 0, k * beta * e_g, jnp.where(row == 1, q * e_g, 0.0))
    ws = _dot(lhs, s0, HIGHEST)
    v_new = v * beta - ws[0:1]
    qk = jnp.sum(q * k, axis=-1, keepdims=True)
    o = ws[1:2] + qk * v_new
    k8 = jnp.where(row == 0, jnp.broadcast_to(k, (8, LANES)), 0.0)
    v8 = jnp.where(row == 0, jnp.broadcast_to(v_new, (8, LANES)), 0.0)
    s_ref[...] = s0 * e_g + _dot_tn(k8, v8, HIGHEST)
    r = lax.rsqrt(jnp.mean(o * o, axis=-1, keepdims=True) + EPS)
    o_ref[...] = ((o * r) * nw_ref[...] * _silu(z_ref[...])).astype(o_ref.dtype)


def sample_gdn(xc, us, z, s0, a_log, dt_bias, norm_w, a_lane, b_lane):
    batch, n_heads = s0.shape[0], s0.shape[1]
    smem = pl.BlockSpec(memory_space=pltpu.SMEM)
    tile = lambda off: pl.BlockSpec((None, None, 1, DK_C), lambda b, h: (b, off + h, 0, 0))
    st = pl.BlockSpec((None, None, DK_C, DK_C), lambda b, h: (b, h, 0, 0))
    return pl.pallas_call(
        functools.partial(_sample_gdn_body, a_lane=a_lane, b_lane=b_lane),
        grid=(batch, n_heads),
        in_specs=[smem, smem, smem, tile(0), tile(n_heads), tile(2 * n_heads), tile(0), st,
                  pl.BlockSpec((1, DK_C), lambda b, h: (0, 0))],
        out_specs=[tile(0), st],
        out_shape=[jax.ShapeDtypeStruct((batch, n_heads, 1, DK_C), BF16), jax.ShapeDtypeStruct(s0.shape, F32)],
        compiler_params=_params("parallel", "parallel"),
        name="sample_gdn",
    )(us, a_log, dt_bias, xc, xc, xc, z, s0, norm_w.reshape(1, DK_C))


def kernel(x_prompt, x_sample, cache_k, cache_v, page_table, state_ssm, state_conv_ssm, state_delta, state_conv_delta, rel_bias_table, norm_w, w_in, lam_q1, lam_k1, lam_q2, lam_k2, subln_w, conv_w_b, conv_bias_b, dt_bias_b, a_log_b, d_skip_b, norm_b_w, conv_w_c, dt_bias_c, a_log_c, norm_c_w, w_branch_a, w_branch_b, w_branch_c, w_out, final_norm_w):
    batch, seq, d = x_prompt.shape
    dec_batch = x_sample.shape[0]
    depth = w_in.shape[0]
    hd = d // 2
    h_a = hd // DV_A
    h_b = hd // P_B
    h_c = hd // DK_C
    conv_b = hd + 2 * G_B * N_B
    conv_c = 3 * hd
    m = batch * seq
    ms = 16

    sizes = (hd, hd, hd, hd, hd, conv_b, h_b, conv_c, hd, h_c, h_c, 3 * d)
    offs = np.concatenate([[0], np.cumsum(sizes)]).tolist()
    o_dt, o_qkvc, o_ac, o_bc, o_gate = offs[6], offs[7], offs[9], offs[10], offs[11]
    c_xbc = 5 * hd
    c_qkvc = c_xbc + conv_b
    c_zc = c_qkvc + conv_c
    c_gate = c_zc + hd
    n_u = c_gate + 3 * d
    bn_u = _pick(math.gcd(n_u, hd), (512, 256, 128))
    regions = ((0, c_qkvc // bn_u, 0), (c_qkvc // bn_u, c_gate // bn_u, o_qkvc - c_qkvc),
               (c_gate // bn_u, n_u // bn_u, o_gate - c_gate))
    blk_dt, blk_ab = o_dt // LANES, o_ac // LANES
    a_lane, b_lane = o_ac % LANES, o_bc % LANES
    assert o_dt % LANES == 0 and h_b <= LANES and b_lane + h_c <= LANES
    assert c_qkvc % bn_u == 0 and c_gate % bn_u == 0 and o_gate - c_gate < LANES

    t_attn = min(ATTN_BLOCK, seq)
    bias = bias_tiles(rel_bias_table, t_attn)
    table2 = jnp.tile(rel_bias_table.T, (2, 1))

    xp = x_prompt.reshape(m, d)
    xs = jnp.zeros((ms, d), F32).at[:dec_batch].set(x_sample.reshape(dec_batch, d))
    k_all = jnp.zeros((depth, m, hd), F32)
    v_all = jnp.zeros((depth, m, hd), F32)
    outs = {k: [] for k in ("ks", "vs", "ssmp", "ssms", "cbp", "cbs", "dp", "ds", "ccp", "ccs")}

    for l in range(depth):
        lam_init = 0.8 - 0.6 * math.exp(-0.3 * l)
        lam = jnp.exp(jnp.sum(lam_q1[l] * lam_k1[l])) - jnp.exp(jnp.sum(lam_q2[l] * lam_k2[l])) + lam_init
        scal = jnp.stack([lam, jnp.asarray(1.0 - lam_init, F32)]).astype(F32)
        zero_bias_c = jnp.zeros((conv_c,), F32)

        hp = rmsnorm(xp, norm_w[l], BF16)
        u, k_all, v_all = input_projection(hp, w_in, l, hd, n_u, regions, BF16, k_all, v_all)
        us_dt, us_ab = small_logits(hp, w_in, l, blk_dt, blk_ab)
        u3 = u.reshape(batch, seq, -1)
        outs["cbp"].append(u3[:, seq - (CONV_K - 1):, c_xbc:c_xbc + conv_b].astype(F32))
        outs["ccp"].append(u3[:, seq - (CONV_K - 1):, c_qkvc:c_qkvc + conv_c].astype(F32))

        o_a = prompt_attention(u, bias, subln_w[l], scal, batch, seq, hd, t_attn)
        xc_b = prompt_conv(u, c_xbc, conv_b, conv_w_b[l], conv_bias_b[l], batch, seq)
        o_b, ssm = prompt_ssd(xc_b, us_dt, u, 4, dt_bias_b[l], a_log_b[l], d_skip_b[l], norm_b_w[l], batch, seq, hd)
        xc_c = prompt_conv(u, c_qkvc, conv_c, conv_w_c[l], zero_bias_c, batch, seq)
        o_c, delta = prompt_gdn(xc_c, us_ab, u, c_zc, a_log_c[l], dt_bias_c[l], norm_c_w[l],
                                batch, seq, hd, a_lane, b_lane)
        outs["ssmp"].append(ssm)
        outs["dp"].append(delta)
        merged = merge_branches(o_a, o_b, o_c, w_branch_a, w_branch_b, w_branch_c, l, u, c_gate)
        xp = out_projection(merged, w_out, l, xp)

        hs = rmsnorm(xs, norm_w[l], BF16)
        u_s = input_projection(hs, w_in, l, hd, n_u, regions, F32)
        us_dt_s, us_ab_s = small_logits(hs, w_in, l, blk_dt, blk_ab)
        ur = u_s[:dec_batch]
        q_s = ur[:, 0:hd].reshape(dec_batch, h_a, DV_A)
        k_s = ur[:, hd:2 * hd].reshape(dec_batch, h_a, DV_A)
        v_s = ur[:, 2 * hd:3 * hd].reshape(dec_batch, h_a, DV_A)
        z_s = ur[:, 3 * hd:4 * hd].reshape(dec_batch, h_a, DV_A)
        outs["ks"].append(k_s.reshape(dec_batch, 1, h_a, DV_A))
        outs["vs"].append(v_s.reshape(dec_batch, 1, h_a, DV_A))
        oa_s = sample_attention(q_s, k_s, v_s, z_s, cache_k, cache_v, l, page_table, table2, subln_w[l], scal)

        xcb_s, cb_state = sample_conv(state_conv_ssm[l], ur[:, c_xbc:c_xbc + conv_b], conv_w_b[l], conv_bias_b[l])
        ob_s, ssm_s = sample_ssd(xcb_s.reshape(dec_batch, 1, conv_b), us_dt_s[:dec_batch].reshape(dec_batch, 1, LANES),
                                 ur[:, 4 * hd:5 * hd].reshape(dec_batch, 1, hd), state_ssm[l],
                                 dt_bias_b[l], a_log_b[l], d_skip_b[l], norm_b_w[l])
        xcc_s, cc_state = sample_conv(state_conv_delta[l], ur[:, c_qkvc:c_qkvc + conv_c], conv_w_c[l], zero_bias_c)
        oc_s, delta_s = sample_gdn(xcc_s.reshape(dec_batch, 3 * h_c, 1, DK_C), us_ab_s,
                                   ur[:, c_zc:c_zc + hd].reshape(dec_batch, h_c, 1, DK_C), state_delta[l],
                                   a_log_c[l], dt_bias_c[l], norm_c_w[l], a_lane, b_lane)
        outs["cbs"].append(cb_state)
        outs["ccs"].append(cc_state)
        outs["ssms"].append(ssm_s)
        outs["ds"].append(delta_s)

        def pad_rows(t):
            return jnp.zeros((ms, hd), BF16).at[:dec_batch].set(t.reshape(dec_batch, hd))

        merged_s = merge_branches(pad_rows(oa_s), pad_rows(ob_s), pad_rows(oc_s),
                                  w_branch_a, w_branch_b, w_branch_c, l, u_s, c_gate)
        xs = out_projection(merged_s, w_out, l, xs)

    y_prompt = rmsnorm(xp, final_norm_w, F32).reshape(batch, seq, d)
    y_sample = rmsnorm(xs, final_norm_w, F32)[:dec_batch].reshape(dec_batch, 1, d)
    st = {k: jnp.stack(v, axis=0) for k, v in outs.items()}
    k_prompt = k_all.reshape(depth, batch, seq, h_a, DV_A)
    v_prompt = v_all.reshape(depth, batch, seq, h_a, DV_A)
    return (y_prompt, y_sample, k_prompt, v_prompt, st["ks"], st["vs"], st["ssmp"], st["ssms"],
            st["cbp"], st["cbs"], st["dp"], st["ds"], st["ccp"], st["ccs"])
```

```python
import functools
import math

import numpy as np
import jax
import jax.numpy as jnp
from jax import lax
from jax.experimental import pallas as pl
from jax.experimental.pallas import tpu as pltpu

F32 = jnp.float32
BF16 = jnp.bfloat16
HIGHEST = lax.Precision.HIGHEST

DK_A = 64
DV_A = 128
NUM_BUCKETS = 32
MAX_DISTANCE = 128
PAGE_SIZE = 128
P_B = 64
G_B = 4
N_B = 128
DK_C = 128
CONV_K = 4
EPS = 1e-6
NEG = -1e30
LOG2E = math.log2(math.e)

LANES = 128
VMEM_LIMIT = 56 * 1024 * 1024

ATTN_BLOCK = 512
SSD_CHUNK = 128
GDN_CHUNK = 64
GDN_SUPER = 256
GDN_HEADS = 2
DECODE_PAGES = 4


def _params(*sem):
    return pltpu.CompilerParams(dimension_semantics=sem, vmem_limit_bytes=VMEM_LIMIT)


def _pick(n, cands):
    for c in cands:
        if n % c == 0:
            return c
    return n


def _silu(x):
    return x * jax.nn.sigmoid(x)


def _softplus(x):
    return jnp.maximum(x, 0.0) + jnp.log1p(jnp.exp(-jnp.abs(x)))


def _dot(a, b, precision=None):
    return jnp.dot(a, b, preferred_element_type=F32, precision=precision)


def _dot_nt(a, b, precision=None):
    return lax.dot_general(a, b, (((1,), (1,)), ((), ())), preferred_element_type=F32, precision=precision)


def _dot_tn(a, b, precision=None):
    return lax.dot_general(a, b, (((0,), (0,)), ((), ())), preferred_element_type=F32, precision=precision)


def _rmsnorm_body(x_ref, w_ref, o_ref):
    x = x_ref[...]
    r = lax.rsqrt(jnp.mean(x * x, axis=-1, keepdims=True) + EPS)
    o_ref[...] = ((x * r) * w_ref[...]).astype(o_ref.dtype)


def rmsnorm(x, w, out_dtype):
    m, d = x.shape
    bm = _pick(m, (256, 128, 64, 32, 16, 8))
    return pl.pallas_call(
        _rmsnorm_body,
        grid=(m // bm,),
        in_specs=[pl.BlockSpec((bm, d), lambda i: (i, 0)), pl.BlockSpec((1, d), lambda i: (0, 0))],
        out_specs=pl.BlockSpec((bm, d), lambda i: (i, 0)),
        out_shape=jax.ShapeDtypeStruct((m, d), out_dtype),
        compiler_params=_params("parallel"),
        name="rmsnorm",
    )(x, w.reshape(1, d))


def _stage_weights(w_ref, e_ref, dst_ref, shift):
    n = dst_ref.shape[0]
    if shift == 0:
        dst_ref[...] = w_ref[...].astype(BF16)
    else:
        dst_ref[0:n - shift, :] = w_ref[shift:n, :].astype(BF16)
        dst_ref[n - shift:n, :] = e_ref[0:shift, :].astype(BF16)


def _inproj_body(x_ref, w_ref, e_ref, *refs, tiles, regions, with_kv):
    if with_kv:
        _, _, o_ref, k_ref, v_ref, wb_ref = refs
    else:
        o_ref, wb_ref = refs
    j = pl.program_id(0)
    i = pl.program_id(1)

    @pl.when(i == 0)
    def _():
        for lo, hi, shift in regions:
            @pl.when((j >= lo) & (j < hi))
            def _():
                _stage_weights(w_ref, e_ref, wb_ref, shift)

    acc = _dot_nt(x_ref[...], wb_ref[...])
    o_ref[...] = acc.astype(o_ref.dtype)
    if with_kv:
        @pl.when((j >= tiles) & (j < 2 * tiles))
        def _():
            k_ref[...] = acc

        @pl.when((j >= 2 * tiles) & (j < 3 * tiles))
        def _():
            v_ref[...] = acc


def input_projection(x, w_t, layer, hd, n_out, regions, out_dtype, k_all=None, v_all=None):
    m, k = x.shape
    bm = _pick(m, (1024, 512, 256, 128, 64, 32, 16))
    bn = _pick(math.gcd(n_out, hd), (512, 256, 128))
    tiles = hd // bn
    nj, ni = n_out // bn, m // bm
    with_kv = k_all is not None
    any_spec = pl.BlockSpec(memory_space=pl.ANY)

    def slab(first):
        def index(j, i):
            row = jnp.where(j < first, 0, jnp.where(j >= first + tiles, ni - 1, i))
            return (layer, row, jnp.clip(j - first, 0, tiles - 1))
        return pl.BlockSpec((None, bm, bn), index)

    in_specs = [pl.BlockSpec((bm, k), lambda j, i: (i, 0)),
                pl.BlockSpec((None, bn, k), lambda j, i: (layer, j, 0)),
                pl.BlockSpec((None, LANES, k), lambda j, i: (layer, (j + 1) * (bn // LANES), 0))]
    out_specs = [pl.BlockSpec((bm, bn), lambda j, i: (i, j))]
    out_shape = [jax.ShapeDtypeStruct((m, n_out), out_dtype)]
    args = [x, w_t, w_t]
    aliases = {}
    if with_kv:
        in_specs += [any_spec, any_spec]
        out_specs += [slab(tiles), slab(2 * tiles)]
        out_shape += [jax.ShapeDtypeStruct(k_all.shape, F32), jax.ShapeDtypeStruct(v_all.shape, F32)]
        args += [k_all, v_all]
        aliases = {3: 1, 4: 2}
    out = pl.pallas_call(
        functools.partial(_inproj_body, tiles=tiles, regions=regions, with_kv=with_kv),
        grid=(nj, ni),
        in_specs=in_specs,
        out_specs=out_specs,
        out_shape=out_shape,
        scratch_shapes=[pltpu.VMEM((bn, k), BF16)],
        input_output_aliases=aliases,
        compiler_params=_params("arbitrary", "arbitrary"),
        name="inproj",
    )(*args)
    return out if with_kv else out[0]


def _small_logits_body(x_ref, wa_ref, wb_ref, oa_ref, ob_ref):
    x = x_ref[...]
    oa_ref[...] = _dot_nt(x, wa_ref[...].astype(BF16))
    ob_ref[...] = _dot_nt(x, wb_ref[...].astype(BF16))


def small_logits(x, w_t, layer, blk_a, blk_b):
    m, k = x.shape
    bm = _pick(m, (1024, 512, 256, 128, 64, 32, 16))
    wspec = lambda blk: pl.BlockSpec((None, LANES, k), lambda i: (layer, blk, 0))
    ospec = pl.BlockSpec((bm, LANES), lambda i: (i, 0))
    return pl.pallas_call(
        _small_logits_body,
        grid=(m // bm,),
        in_specs=[pl.BlockSpec((bm, k), lambda i: (i, 0)), wspec(blk_a), wspec(blk_b)],
        out_specs=[ospec, ospec],
        out_shape=[jax.ShapeDtypeStruct((m, LANES), F32)] * 2,
        compiler_params=_params("parallel"),
        name="small_logits",
    )(x, w_t, w_t)


def _merge_body(oa_ref, ob_ref, oc_ref, wa_ref, wb_ref, wc_ref, ga_ref, gb_ref, gc_ref, o_ref, sa, sb, sc):
    @pl.when(pl.program_id(1) == 0)
    def _():
        sa[...] = wa_ref[...].astype(BF16)
        sb[...] = wb_ref[...].astype(BF16)
        sc[...] = wc_ref[...].astype(BF16)

    acc = jax.nn.sigmoid(ga_ref[...].astype(F32)) * _dot(oa_ref[...], sa[...])
    acc = acc + jax.nn.sigmoid(gb_ref[...].astype(F32)) * _dot(ob_ref[...], sb[...])
    acc = acc + jax.nn.sigmoid(gc_ref[...].astype(F32)) * _dot(oc_ref[...], sc[...])
    o_ref[...] = acc.astype(o_ref.dtype)


def merge_branches(o_a, o_b, o_c, w_a, w_b, w_c, layer, u, gate_off):
    m, hd = o_a.shape
    d = w_a.shape[2]
    bm = _pick(m, (512, 256, 128, 64, 32, 16))
    bn = _pick(math.gcd(d, gate_off), (512, 256, 128))
    g0 = gate_off // bn
    nd = d // bn
    row = pl.BlockSpec((bm, hd), lambda j, i: (i, 0))
    wsp = pl.BlockSpec((None, hd, bn), lambda j, i: (layer, 0, j))

    def gate(t):
        return pl.BlockSpec((bm, bn), lambda j, i: (i, g0 + t * nd + j))

    return pl.pallas_call(
        _merge_body,
        grid=(nd, m // bm),
        in_specs=[row, row, row, wsp, wsp, wsp, gate(0), gate(1), gate(2)],
        out_specs=pl.BlockSpec((bm, bn), lambda j, i: (i, j)),
        out_shape=jax.ShapeDtypeStruct((m, d), BF16),
        scratch_shapes=[pltpu.VMEM((hd, bn), BF16)] * 3,
        compiler_params=_params("parallel", "arbitrary"),
        name="merge",
    )(o_a, o_b, o_c, w_a, w_b, w_c, u, u, u)


def _outproj_body(m_ref, w_ref, x_ref, o_ref, ws):
    @pl.when(pl.program_id(1) == 0)
    def _():
        ws[...] = w_ref[...].astype(BF16)

    o_ref[...] = x_ref[...] + _dot(m_ref[...], ws[...])


def out_projection(merged, w_out, layer, x):
    m, d = merged.shape
    bm = _pick(m, (1024, 512, 256, 128, 64, 32, 16))
    bn = _pick(d, (512, 256, 128))
    return pl.pallas_call(
        _outproj_body,
        grid=(d // bn, m // bm),
        in_specs=[pl.BlockSpec((bm, d), lambda j, i: (i, 0)), pl.BlockSpec((None, d, bn), lambda j, i: (layer, 0, j)),
                  pl.BlockSpec((bm, bn), lambda j, i: (i, j))],
        out_specs=pl.BlockSpec((bm, bn), lambda j, i: (i, j)),
        out_shape=jax.ShapeDtypeStruct((m, d), F32),
        scratch_shapes=[pltpu.VMEM((d, bn), BF16)],
        compiler_params=_params("parallel", "arbitrary"),
        name="outproj",
    )(merged, w_out, x)


def _bucket_changes():
    max_exact = NUM_BUCKETS // 2
    n = np.arange(0, MAX_DISTANCE + 1)
    nf = np.maximum(n, 1).astype(np.float32)
    large = max_exact + (np.log(nf / np.float32(max_exact)) / np.float32(math.log(MAX_DISTANCE / max_exact))
                         * np.float32(NUM_BUCKETS - max_exact)).astype(np.int32)
    bucket = np.where(n < max_exact, n, np.minimum(large, NUM_BUCKETS - 1))
    bucket[MAX_DISTANCE] = NUM_BUCKETS - 1
    changes = [(0, int(bucket[0]))]
    for d in range(1, MAX_DISTANCE + 1):
        if bucket[d] != bucket[d - 1]:
            changes.append((d, int(bucket[d])))
    return changes


def _bias_tiles_body(tbl_ref, o_ref, *, t, n_heads):
    h = pl.program_id(0)
    i = lax.broadcasted_iota(jnp.int32, (t, t), 0)
    j = lax.broadcasted_iota(jnp.int32, (t, t), 1)
    last = tbl_ref[(NUM_BUCKETS - 1) * n_heads + h]
    changes = _bucket_changes()
    for tile, off in ((0, 0), (1, t)):
        d = i - j + off
        val = jnp.full((t, t), (tbl_ref[changes[0][1] * n_heads + h] - last) * LOG2E, F32)
        for ds, b in changes[1:]:
            val = jnp.where(d >= ds, (tbl_ref[b * n_heads + h] - last) * LOG2E, val)
        if off == 0:
            val = jnp.where(j <= i, val, NEG)
        o_ref[tile] = val


def bias_tiles(table, t):
    n_heads = table.shape[1]
    return pl.pallas_call(
        functools.partial(_bias_tiles_body, t=t, n_heads=n_heads),
        grid=(n_heads,),
        in_specs=[pl.BlockSpec(memory_space=pltpu.SMEM)],
        out_specs=pl.BlockSpec((None, 2, t, t), lambda h: (h, 0, 0, 0)),
        out_shape=jax.ShapeDtypeStruct((n_heads, 2, t, t), F32),
        compiler_params=_params("parallel"),
        name="bias_tiles",
    )(table.reshape(-1))


def _attn_body(scal_ref, q_ref, kb_ref, vb_ref, z_ref, bias_ref, sw_ref, o_ref, *, t):
    qi = pl.program_id(2)
    q = q_ref[...].astype(F32) * (DK_A ** -0.5 * LOG2E)
    lane = lax.broadcasted_iota(jnp.int32, q.shape, 1)
    qq = jnp.concatenate([jnp.where(lane < DK_A, q, 0.0), jnp.where(lane >= DK_A, q, 0.0)], axis=0).astype(BF16)

    def step(kc, carry, bias_idx):
        m, l, acc = carry
        start = pl.multiple_of(kc * t, t)
        s = _dot_nt(qq, kb_ref[pl.ds(start, t), :])
        if bias_idx is not None:
            s = (s.reshape(2, t, t) + bias_ref[bias_idx][None]).reshape(2 * t, t)
        m_new = jnp.maximum(m, jnp.max(s, axis=-1, keepdims=True))
        alpha = jnp.exp2(m - m_new)
        p = jnp.exp2(s - m_new)
        l = alpha * l + jnp.sum(p, axis=-1, keepdims=True)
        acc = alpha * acc + _dot(p.astype(BF16), vb_ref[pl.ds(start, t), :])
        return m_new, l, acc

    carry = (jnp.full((2 * t, 1), NEG, F32), jnp.zeros((2 * t, 1), F32), jnp.zeros((2 * t, DV_A), F32))
    far = jnp.maximum(qi - 1, 0)
    carry = lax.fori_loop(0, far, lambda kc, c: step(kc, c, None), carry)
    carry = lax.fori_loop(far, qi, lambda kc, c: step(kc, c, 1), carry)
    _, l, acc = step(qi, carry, 0)

    o = acc / l
    o = o[:t] - scal_ref[0] * o[t:]
    r = lax.rsqrt(jnp.mean(o * o, axis=-1, keepdims=True) + EPS)
    o = (o * r) * sw_ref[...] * scal_ref[1]
    o_ref[...] = (o * _silu(z_ref[...].astype(F32))).astype(o_ref.dtype)


def prompt_attention(u, bias, subln_w, scal, batch, seq, hd, t):
    n_heads = hd // DV_A
    nq = seq // t
    return pl.pallas_call(
        functools.partial(_attn_body, t=t),
        grid=(batch, n_heads, nq),
        in_specs=[
            pl.BlockSpec(memory_space=pltpu.SMEM),
            pl.BlockSpec((t, DV_A), lambda b, h, qi: (b * nq + qi, h)),
            pl.BlockSpec((seq, DV_A), lambda b, h, qi: (b, n_heads + h)),
            pl.BlockSpec((seq, DV_A), lambda b, h, qi: (b, 2 * n_heads + h)),
            pl.BlockSpec((t, DV_A), lambda b, h, qi: (b * nq + qi, 3 * n_heads + h)),
            pl.BlockSpec((None, 2, t, t), lambda b, h, qi: (h, 0, 0, 0)),
            pl.BlockSpec((1, DV_A), lambda b, h, qi: (0, 0)),
        ],
        out_specs=pl.BlockSpec((t, DV_A), lambda b, h, qi: (b * nq + qi, h)),
        out_shape=jax.ShapeDtypeStruct((batch * seq, hd), BF16),
        compiler_params=_params("parallel", "parallel", "arbitrary"),
        name="prompt_attn",
    )(scal, u, u, u, u, bias, subln_w.reshape(1, DV_A))


def _conv_body(x_ref, w_ref, b_ref, o_ref):
    x = x_ref[...].astype(F32)
    row = lax.broadcasted_iota(jnp.int32, x.shape, 0)
    acc = x * w_ref[CONV_K - 1:CONV_K, :]
    for k in range(1, CONV_K):
        shifted = jnp.where(row >= k, pltpu.roll(x, k, axis=0), 0.0)
        acc = acc + shifted * w_ref[CONV_K - 1 - k:CONV_K - k, :]
    acc = acc + b_ref[...]
    o_ref[...] = _silu(acc)


def prompt_conv(u, col_off, width, w, bias, batch, seq):
    cw = _pick(math.gcd(col_off, width), (512, 256, 128))
    c0 = col_off // cw
    return pl.pallas_call(
        _conv_body,
        grid=(batch, width // cw),
        in_specs=[pl.BlockSpec((seq, cw), lambda b, c: (b, c0 + c)),
                  pl.BlockSpec((CONV_K, cw), lambda b, c: (0, c)),
                  pl.BlockSpec((1, cw), lambda b, c: (0, c))],
        out_specs=pl.BlockSpec((seq, cw), lambda b, c: (b, c)),
        out_shape=jax.ShapeDtypeStruct((batch * seq, width), F32),
        compiler_params=_params("parallel", "parallel"),
        name="prompt_conv",
    )(u, w, bias.reshape(1, width))


def _sample_conv_body(prev_ref, x_ref, w_ref, b_ref, y_ref, st_ref):
    x = x_ref[...]
    acc = x * w_ref[CONV_K - 1:CONV_K, :]
    for i in range(CONV_K - 1):
        acc = acc + prev_ref[i] * w_ref[i:i + 1, :]
    y_ref[...] = _silu(acc + b_ref[...])
    for i in range(CONV_K - 2):
        st_ref[i] = prev_ref[i + 1]
    st_ref[CONV_K - 2] = x


def sample_conv(prev, x, w, bias):
    rows, c = x.shape
    prev_t = jnp.transpose(prev, (1, 0, 2))
    cw = _pick(c, (1024, 512, 256, 128))
    y, st = pl.pallas_call(
        _sample_conv_body,
        grid=(c // cw,),
        in_specs=[pl.BlockSpec((CONV_K - 1, rows, cw), lambda j: (0, 0, j)),
                  pl.BlockSpec((rows, cw), lambda j: (0, j)),
                  pl.BlockSpec((CONV_K, cw), lambda j: (0, j)),
                  pl.BlockSpec((1, cw), lambda j: (0, j))],
        out_specs=[pl.BlockSpec((rows, cw), lambda j: (0, j)),
                   pl.BlockSpec((CONV_K - 1, rows, cw), lambda j: (0, 0, j))],
        out_shape=[jax.ShapeDtypeStruct((rows, c), F32), jax.ShapeDtypeStruct((CONV_K - 1, rows, c), F32)],
        compiler_params=_params("parallel"),
        name="sample_conv",
    )(prev_t, x, w, bias.reshape(1, c))
    return y, jnp.transpose(st, (1, 0, 2))


def _ssd_body(x_ref, us_ref, z_ref, dtb_ref, alog_ref, dskip_ref, nw_ref, o_ref, h_ref, *, t, hd, n_heads):
    c = pl.program_id(1)

    @pl.when(c == 0)
    def _():
        h_ref[...] = jnp.zeros_like(h_ref)

    heads_per_group = n_heads // G_B
    dt = _softplus(us_ref[:, 0:n_heads] + dtb_ref[...])
    a = dt * (-jnp.exp(alog_ref[...]))
    ri = lax.broadcasted_iota(jnp.int32, (t, t), 0)
    ci = lax.broadcasted_iota(jnp.int32, (t, t), 1)
    lower = ci <= ri
    a_col = _dot(lower.astype(F32), a, HIGHEST)
    a_row = _dot_tn(a, (ri <= ci).astype(F32), HIGHEST)
    a_last = a_col[t - 1:t, :]
    e_col = jnp.exp(a_col)
    e_end = jnp.exp(a_last - a_col)
    e_last = jnp.exp(a_last)
    lane = lax.broadcasted_iota(jnp.int32, (t, LANES), 1)
    lo = lane < P_B
    row_lo = lax.broadcasted_iota(jnp.int32, (2 * P_B, N_B), 0) < P_B
    dskip = dskip_ref[...]

    cb = {}
    ys = []
    for j in range(n_heads // 2):
        ha, hb = 2 * j, 2 * j + 1
        g = ha // heads_per_group
        bg = x_ref[:, hd + g * N_B:hd + (g + 1) * N_B].astype(BF16)
        cg = x_ref[:, hd + (G_B + g) * N_B:hd + (G_B + g + 1) * N_B].astype(BF16)
        if g not in cb:
            cb[g] = _dot_nt(cg, bg)
        x = x_ref[:, j * LANES:(j + 1) * LANES]
        xdt = x * jnp.where(lo, dt[:, ha:ha + 1], dt[:, hb:hb + 1])
        xdt_b = xdt.astype(BF16)
        y = None
        for hh, keep in ((ha, lo), (hb, ~lo)):
            seg = a_col[:, hh:hh + 1] - a_row[hh:hh + 1, :]
            w = (cb[g] * jnp.exp(jnp.where(lower, seg, NEG))).astype(BF16)
            part = _dot(w, jnp.where(keep, xdt_b, jnp.zeros_like(xdt_b)))
            y = part if y is None else y + part
        h_prev = h_ref[ha:hb + 1].reshape(2 * P_B, N_B)
        y = y + _dot_nt(cg, h_prev.astype(BF16)) * jnp.where(lo, e_col[:, ha:ha + 1], e_col[:, hb:hb + 1])
        xdec = (xdt * jnp.where(lo, e_end[:, ha:ha + 1], e_end[:, hb:hb + 1])).astype(BF16)
        st = _dot_tn(xdec, bg)
        h_new = h_prev * jnp.where(row_lo, e_last[:, ha:ha + 1], e_last[:, hb:hb + 1]) + st
        h_ref[ha:hb + 1] = h_new.reshape(2, P_B, N_B)
        y = y + x * jnp.where(lo[:1], dskip[:, ha:ha + 1], dskip[:, hb:hb + 1])
        ys.append(y)
    y = jnp.concatenate(ys, axis=1)
    y = y * _silu(z_ref[...].astype(F32))
    r = lax.rsqrt(jnp.mean(y * y, axis=-1, keepdims=True) + EPS)
    o_ref[...] = ((y * r) * nw_ref[...]).astype(o_ref.dtype)


def prompt_ssd(xc, us, u, z_blk, dt_bias, a_log, d_skip, norm_w, batch, seq, hd):
    n_heads = hd // P_B
    t = min(SSD_CHUNK, seq)
    nc = seq // t
    cdim = xc.shape[1]
    vec = pl.BlockSpec((1, n_heads), lambda b, c: (0, 0))
    return pl.pallas_call(
        functools.partial(_ssd_body, t=t, hd=hd, n_heads=n_heads),
        grid=(batch, nc),
        in_specs=[pl.BlockSpec((t, cdim), lambda b, c: (b * nc + c, 0)),
                  pl.BlockSpec((t, LANES), lambda b, c: (b * nc + c, 0)),
                  pl.BlockSpec((t, hd), lambda b, c: (b * nc + c, z_blk)),
                  vec, vec, vec,
                  pl.BlockSpec((1, hd), lambda b, c: (0, 0))],
        out_specs=[pl.BlockSpec((t, hd), lambda b, c: (b * nc + c, 0)),
                   pl.BlockSpec((None, n_heads, P_B, N_B), lambda b, c: (b, 0, 0, 0))],
        out_shape=[jax.ShapeDtypeStruct((batch * seq, hd), BF16),
                   jax.ShapeDtypeStruct((batch, n_heads, P_B, N_B), F32)],
        compiler_params=_params("parallel", "arbitrary"),
        name="prompt_ssd",
    )(xc, us, u, dt_bias.reshape(1, -1), a_log.reshape(1, -1), d_skip.reshape(1, -1), norm_w.reshape(1, hd))


def _l2norm(x):
    return x * lax.rsqrt(jnp.sum(x * x, axis=-1, keepdims=True) + EPS)


def _split3(x):
    a = x.astype(BF16).astype(F32)
    r = x - a
    b = r.astype(BF16).astype(F32)
    return a, b, (r - b).astype(BF16).astype(F32)


def _gdn_body(alog_ref, dtb_ref, q_ref, k_ref, v_ref, z_ref, us_ref, nw_ref, o_ref, s_ref,
              u_s, wq_s, qk_s, kd_s, gl_s, *, tc, sb, nc, a_lane, b_lane):
    pid = pl.program_id(1)
    cps = sb // tc
    shift = tc.bit_length() - 1
    ri = lax.broadcasted_iota(jnp.int32, (sb, sb), 0)
    ci = lax.broadcasted_iota(jnp.int32, (sb, sb), 1)
    same = lax.shift_right_logical(ri, shift) == lax.shift_right_logical(ci, shift)
    incl = same & (ci <= ri)
    strict = same & (ci < ri)
    eye = jnp.where(ri == ci, 1.0, 0.0)
    tri = jnp.where(incl, 1.0, 0.0).astype(BF16)
    tri_t = jnp.where(same & (ri <= ci), 1.0, 0.0).astype(BF16)
    tri_ones = jnp.concatenate([tri, jnp.where(same, 1.0, 0.0).astype(BF16)], axis=0)
    lane = lax.broadcasted_iota(jnp.int32, (sb, LANES), 1)

    def sum3(x):
        return x[:, 0:1] + x[:, 1:2] + x[:, 2:3]

    def prepare(sidx, hh):
        h = pid * GDN_HEADS + hh
        rows = pl.ds(pl.multiple_of(sidx * sb, sb), sb)
        cols = slice(hh * DK_C, (hh + 1) * DK_C)
        us = us_ref[rows, :]
        a_col = jnp.sum(jnp.where(lane == a_lane + h, us, 0.0), axis=-1, keepdims=True)
        b_col = jnp.sum(jnp.where(lane == b_lane + h, us, 0.0), axis=-1, keepdims=True)
        g = -jnp.exp(jnp.full((1, 1), alog_ref[h], F32)) * _softplus(a_col + dtb_ref[h])
        beta = jax.nn.sigmoid(b_col)
        g1, g2, g3 = _split3(g)
        gm = jnp.where(lane == 0, g1, jnp.where(lane == 1, g2, jnp.where(lane == 2, g3, 0.0))).astype(BF16)
        cum = _dot(tri_ones, gm)
        gc = sum3(cum[:sb])
        g_end = sum3(cum[sb:])
        cum_t = _dot_tn(gm, tri_t)
        gc_row = cum_t[0:1] + cum_t[1:2] + cum_t[2:3]
        decay = jnp.exp(jnp.where(incl, gc - gc_row, NEG))
        q = _l2norm(q_ref[rows, cols]) * (DK_C ** -0.5)
        k = _l2norm(k_ref[rows, cols])
        kb = k * beta
        k_b = k.astype(BF16)
        kq = _dot_nt(jnp.concatenate([kb, q], axis=0).astype(BF16), k_b)
        m = jnp.where(strict, kq[:sb] * decay, 0.0)
        x = eye - m
        p = m.astype(BF16)
        n = 2
        while n <= tc // 2:
            p = _dot(p, p).astype(BF16)
            x = x + _dot(x.astype(BF16), p)
            n *= 2
        a_mat = eye + m
        a_hi = a_mat.astype(BF16)
        a_lo = (a_mat - a_hi.astype(F32)).astype(BF16)
        x_hi = x.astype(BF16)
        x_lo = (x - x_hi.astype(F32)).astype(BF16)
        y = _dot(jnp.concatenate([a_hi, a_lo], axis=0), x_hi)
        resid = eye - (y[:sb] + y[sb:] + _dot(a_hi, x_lo))
        t_inv = (x + _dot(x_hi, resid.astype(BF16))).astype(BF16)
        e_gc = jnp.exp(gc)
        rhs = jnp.concatenate([v_ref[rows, cols] * beta, kb * e_gc], axis=1).astype(BF16)
        uw = _dot(t_inv, rhs)
        u_s[hh, rows, :] = uw[:, :DK_C]
        w = uw[:, DK_C:].astype(BF16)
        qd = (q * e_gc).astype(BF16)
        qk = (kq[sb:] * decay).astype(BF16)
        kd_s[hh, rows, :] = (k * jnp.exp(g_end - gc)).astype(BF16)
        e_end = jnp.exp(g_end)
        for c in range(cps):
            cidx = sidx * cps + c
            r0 = c * tc
            wq_s[hh, cidx, 0:tc, :] = w[r0:r0 + tc]
            wq_s[hh, cidx, tc:2 * tc, :] = qd[r0:r0 + tc]
            qk_s[hh, cidx] = qk[r0:r0 + tc, r0:r0 + tc]
            gl_s[hh, pl.ds(cidx, 1), :] = jnp.broadcast_to(e_end[r0:r0 + 1, :], (1, LANES))

    def prepare_all(sidx, carry):
        for hh in range(GDN_HEADS):
            prepare(sidx, hh)
        return carry

    lax.fori_loop(0, nc // cps, prepare_all, 0)

    def recur(cidx, states):
        rows = pl.ds(pl.multiple_of(cidx * tc, tc), tc)
        new = []
        for hh in range(GDN_HEADS):
            cols = slice(hh * DK_C, (hh + 1) * DK_C)
            s = states[hh]
            s_b = s.astype(BF16)
            ws = _dot(wq_s[hh, cidx], s_b)
            v_b = (u_s[hh, rows, :] - ws[:tc]).astype(BF16)
            o = ws[tc:] + _dot(qk_s[hh, cidx], v_b)
            r = lax.rsqrt(jnp.mean(o * o, axis=-1, keepdims=True) + EPS)
            o_ref[rows, cols] = ((o * r) * nw_ref[...] * _silu(z_ref[rows, cols].astype(F32))).astype(o_ref.dtype)
            new.append(s * gl_s[hh, pl.ds(cidx, 1), :] + _dot_tn(kd_s[hh, rows, :], v_b))
        return tuple(new)

    final = lax.fori_loop(0, nc, recur, tuple(jnp.zeros((DK_C, DK_C), F32) for _ in range(GDN_HEADS)))
    for hh in range(GDN_HEADS):
        s_ref[hh] = final[hh]


def prompt_gdn(xc, us, u, z_col, a_log, dt_bias, norm_w, batch, seq, hd, a_lane, b_lane):
    n_heads = hd // DK_C
    tc = min(GDN_CHUNK, seq)
    sb = min(GDN_SUPER, seq)
    nc = seq // tc
    gw = GDN_HEADS * DK_C
    smem = pl.BlockSpec(memory_space=pltpu.SMEM)
    heads = lambda col: pl.BlockSpec((seq, gw), lambda b, h: (b, col // gw + h))
    return pl.pallas_call(
        functools.partial(_gdn_body, tc=tc, sb=sb, nc=nc, a_lane=a_lane, b_lane=b_lane),
        grid=(batch, n_heads // GDN_HEADS),
        in_specs=[smem, smem, heads(0), heads(hd), heads(2 * hd), heads(z_col),
                  pl.BlockSpec((seq, LANES), lambda b, h: (b, 0)),
                  pl.BlockSpec((1, DK_C), lambda b, h: (0, 0))],
        out_specs=[pl.BlockSpec((seq, gw), lambda b, h: (b, h)),
                   pl.BlockSpec((None, GDN_HEADS, DK_C, DK_C), lambda b, h: (b, h, 0, 0))],
        out_shape=[jax.ShapeDtypeStruct((batch * seq, hd), BF16),
                   jax.ShapeDtypeStruct((batch, n_heads, DK_C, DK_C), F32)],
        scratch_shapes=[pltpu.VMEM((GDN_HEADS, seq, DK_C), F32),
                        pltpu.VMEM((GDN_HEADS, nc, 2 * tc, DK_C), BF16),
                        pltpu.VMEM((GDN_HEADS, nc, tc, tc), BF16),
                        pltpu.VMEM((GDN_HEADS, seq, DK_C), BF16),
                        pltpu.VMEM((GDN_HEADS, nc, LANES), F32)],
        compiler_params=_params("parallel", "parallel"),
        name="prompt_gdn",
    )(a_log, dt_bias, xc, xc, xc, u, us, norm_w.reshape(1, DK_C))


def _sample_attn_body(pt_ref, scal_ref, q_ref, kn_ref, vn_ref, z_ref, tbl_ref, sw_ref, *refs,
                      n_heads, n_steps, pages):
    k_refs, v_refs = refs[:pages], refs[pages:2 * pages]
    o_ref, m_s, l_s, acc_s = refs[2 * pages:]
    p = pl.program_id(1)
    rows = 2 * n_heads
    cols = PAGE_SIZE * n_heads

    @pl.when(p == 0)
    def _():
        m_s[...] = jnp.full_like(m_s, NEG)
        l_s[...] = jnp.zeros_like(l_s)
        acc_s[...] = jnp.zeros_like(acc_s)

    q = q_ref[...] * (DK_A ** -0.5)
    lane = lax.broadcasted_iota(jnp.int32, q.shape, 1)
    qq = jnp.concatenate([jnp.where(lane < DK_A, q, 0.0), jnp.where(lane >= DK_A, q, 0.0)], axis=0).astype(BF16)
    r_i = lax.broadcasted_iota(jnp.int32, (rows, cols), 0)
    c_i = lax.broadcasted_iota(jnp.int32, (rows, cols), 1)
    own = (c_i % n_heads) == (r_i % n_heads)
    last = tbl_ref[:, NUM_BUCKETS - 1:NUM_BUCKETS]
    changes = _bucket_changes()
    delta0 = tbl_ref[:, changes[0][1]:changes[0][1] + 1] - last

    def add_bias(s):
        d = PAGE_SIZE - c_i // n_heads
        val = jnp.broadcast_to(delta0, (rows, cols))
        for ds, b in changes[1:]:
            val = jnp.where(d >= ds, tbl_ref[:, b:b + 1] - last, val)
        return s + val

    ss = [_dot_nt(qq, k_refs[i][...].reshape(cols, DV_A).astype(BF16)) for i in range(pages)]
    ss[-1] = lax.cond(p == n_steps - 1, add_bias, lambda s: s, ss[-1])
    ss = [jnp.where(own, s, NEG) for s in ss]
    m_new = m_s[...]
    for s in ss:
        m_new = jnp.maximum(m_new, jnp.max(s, axis=-1, keepdims=True))
    alpha = jnp.exp(m_s[...] - m_new)
    l_new = alpha * l_s[...]
    acc = alpha * acc_s[...]
    for i, s in enumerate(ss):
        pr = jnp.where(own, jnp.exp(s - m_new), 0.0)
        l_new = l_new + jnp.sum(pr, axis=-1, keepdims=True)
        acc = acc + _dot(pr.astype(BF16), v_refs[i][...].reshape(cols, DV_A).astype(BF16))
    m_s[...] = m_new
    l_s[...] = l_new
    acc_s[...] = acc

    @pl.when(p == n_steps - 1)
    def _():
        kn = jnp.concatenate([kn_ref[...], kn_ref[...]], axis=0).astype(BF16).astype(F32)
        vn = jnp.concatenate([vn_ref[...], vn_ref[...]], axis=0).astype(BF16).astype(F32)
        s_new = jnp.sum(qq.astype(F32) * kn, axis=-1, keepdims=True) + delta0
        m_fin = jnp.maximum(m_new, s_new)
        a2 = jnp.exp(m_new - m_fin)
        p_new = jnp.exp(s_new - m_fin)
        l = a2 * l_new + p_new
        o = (a2 * acc + p_new.astype(BF16).astype(F32) * vn) / l
        o = o[:n_heads] - scal_ref[0] * o[n_heads:]
        r = lax.rsqrt(jnp.mean(o * o, axis=-1, keepdims=True) + EPS)
        o = (o * r) * sw_ref[...] * scal_ref[1]
        o_ref[...] = (o * _silu(z_ref[...])).astype(o_ref.dtype)


def sample_attention(q, k_new, v_new, z, cache_k, cache_v, layer, page_table, table2, subln_w, scal):
    batch, n_heads, _ = q.shape
    n_pages = page_table.shape[1]
    pages = _pick(n_pages, (DECODE_PAGES, 2, 1))
    n_steps = n_pages // pages
    per_seq = pl.BlockSpec((None, n_heads, DV_A), lambda b, p, pt: (b, 0, 0))

    def page(i):
        return pl.BlockSpec((None, None, PAGE_SIZE, n_heads, DV_A),
                            lambda b, p, pt: (layer, pt[b * n_pages + p * pages + i], 0, 0, 0))

    grid_spec = pltpu.PrefetchScalarGridSpec(
        num_scalar_prefetch=1,
        grid=(batch, n_steps),
        in_specs=[pl.BlockSpec(memory_space=pltpu.SMEM), per_seq, per_seq, per_seq, per_seq,
                  pl.BlockSpec((2 * n_heads, NUM_BUCKETS), lambda b, p, pt: (0, 0)),
                  pl.BlockSpec((1, DV_A), lambda b, p, pt: (0, 0))]
                 + [page(i) for i in range(pages)] + [page(i) for i in range(pages)],
        out_specs=per_seq,
        scratch_shapes=[pltpu.VMEM((2 * n_heads, 1), F32), pltpu.VMEM((2 * n_heads, 1), F32),
                        pltpu.VMEM((2 * n_heads, DV_A), F32)],
    )
    return pl.pallas_call(
        functools.partial(_sample_attn_body, n_heads=n_heads, n_steps=n_steps, pages=pages),
        grid_spec=grid_spec,
        out_shape=jax.ShapeDtypeStruct((batch, n_heads, DV_A), BF16),
        compiler_params=_params("parallel", "arbitrary"),
        name="sample_attn",
    )(page_table.reshape(-1), scal, q, k_new, v_new, z, table2, subln_w.reshape(1, DV_A),
      *([cache_k] * pages), *([cache_v] * pages))


def _sample_ssd_body(x_ref, us_ref, z_ref, h0_ref, dtb_ref, alog_ref, dskip_ref, nw_ref, o_ref, h_ref, *, hd, n_heads):
    heads_per_group = n_heads // G_B
    dt = _softplus(us_ref[:, 0:n_heads] + dtb_ref[...])
    e_a = jnp.exp(dt * (-jnp.exp(alog_ref[...])))
    lane = lax.broadcasted_iota(jnp.int32, (1, LANES), 1)
    lo = lane < P_B
    row_lo = lax.broadcasted_iota(jnp.int32, (2 * P_B, N_B), 0) < P_B
    first = lax.broadcasted_iota(jnp.int32, (8, LANES), 0) == 0
    dskip = dskip_ref[...]
    ys = []
    for j in range(n_heads // 2):
        ha, hb = 2 * j, 2 * j + 1
        g = ha // heads_per_group
        bg = x_ref[:, hd + g * N_B:hd + (g + 1) * N_B]
        cg = x_ref[:, hd + (G_B + g) * N_B:hd + (G_B + g + 1) * N_B]
        cg_r = cg.astype(BF16).astype(F32)
        cb = jnp.sum(cg_r * bg.astype(BF16).astype(F32), axis=-1, keepdims=True)
        x = x_ref[:, j * LANES:(j + 1) * LANES]
        xdt = x * jnp.where(lo, dt[:, ha:ha + 1], dt[:, hb:hb + 1])
        h_prev = h0_ref[ha:hb + 1].reshape(2 * P_B, N_B)
        c8 = jnp.where(first, jnp.broadcast_to(cg, (8, N_B)), 0.0).astype(BF16)
        y_off = _dot_nt(c8, h_prev.astype(BF16))[0:1] * jnp.where(lo, e_a[:, ha:ha + 1], e_a[:, hb:hb + 1])
        x8 = jnp.where(first, jnp.broadcast_to(xdt, (8, LANES)), 0.0)
        b8 = jnp.where(first, jnp.broadcast_to(bg, (8, N_B)), 0.0)
        st = _dot_tn(x8, b8, HIGHEST)
        h_new = h_prev * jnp.where(row_lo, e_a[:, ha:ha + 1], e_a[:, hb:hb + 1]) + st
        h_ref[ha:hb + 1] = h_new.reshape(2, P_B, N_B)
        ys.append(cb * xdt + y_off + x * jnp.where(lo, dskip[:, ha:ha + 1], dskip[:, hb:hb + 1]))
    y = jnp.concatenate(ys, axis=1) * _silu(z_ref[...])
    r = lax.rsqrt(jnp.mean(y * y, axis=-1, keepdims=True) + EPS)
    o_ref[...] = ((y * r) * nw_ref[...]).astype(o_ref.dtype)


def sample_ssd(xc, us, z, h0, dt_bias, a_log, d_skip, norm_w):
    batch, n_heads = h0.shape[0], h0.shape[1]
    hd = n_heads * P_B
    cdim = xc.shape[2]
    rowspec = lambda w: pl.BlockSpec((None, 1, w), lambda b: (b, 0, 0))
    vec = pl.BlockSpec((1, n_heads), lambda b: (0, 0))
    st = pl.BlockSpec((None, n_heads, P_B, N_B), lambda b: (b, 0, 0, 0))
    return pl.pallas_call(
        functools.partial(_sample_ssd_body, hd=hd, n_heads=n_heads),
        grid=(batch,),
        in_specs=[rowspec(cdim), rowspec(LANES), rowspec(hd), st, vec, vec, vec, pl.BlockSpec((1, hd), lambda b: (0, 0))],
        out_specs=[rowspec(hd), st],
        out_shape=[jax.ShapeDtypeStruct((batch, 1, hd), BF16), jax.ShapeDtypeStruct(h0.shape, F32)],
        compiler_params=_params("parallel"),
        name="sample_ssd",
    )(xc, us, z, h0, dt_bias.reshape(1, -1), a_log.reshape(1, -1), d_skip.reshape(1, -1), norm_w.reshape(1, hd))


def _sample_gdn_body(us_ref, alog_ref, dtb_ref, q_ref, k_ref, v_ref, z_ref, s0_ref, nw_ref, o_ref, s_ref,
                     *, a_lane, b_lane):
    b = pl.program_id(0)
    h = pl.program_id(1)
    a_logit = jnp.full((1, LANES), us_ref[b, a_lane + h], F32)
    b_logit = jnp.full((1, LANES), us_ref[b, b_lane + h], F32)
    g = -jnp.exp(jnp.full((1, LANES), alog_ref[h], F32)) * _softplus(a_logit + dtb_ref[h])
    e_g = jnp.exp(g)
    beta = jax.nn.sigmoid(b_logit)
    q = _l2norm(q_ref[...]) * (DK_C ** -0.5)
    k = _l2norm(k_ref[...])
    v = v_ref[...]
    s0 = s0_ref[...]
    row = lax.broadcasted_iota(jnp.int32, (8, LANES), 0)
    lhs = jnp.where(row == 0, k * beta * e_g, jnp.where(row == 1, q * e_g, 0.0))
    ws = _dot(lhs, s0, HIGHEST)
    v_new = v * beta - ws[0:1]
    qk = jnp.sum(q * k, axis=-1, keepdims=True)
    o = ws[1:2] + qk * v_new
    k8 = jnp.where(row == 0, jnp.broadcast_to(k, (8, LANES)), 0.0)
    v8 = jnp.where(row == 0, jnp.broadcast_to(v_new, (8, LANES)), 0.0)
    s_ref[...] = s0 * e_g + _dot_tn(k8, v8, HIGHEST)
    r = lax.rsqrt(jnp.mean(o * o, axis=-1, keepdims=True) + EPS)
    o_ref[...] = ((o * r) * nw_ref[...] * _silu(z_ref[...])).astype(o_ref.dtype)


def sample_gdn(xc, us, z, s0, a_log, dt_bias, norm_w, a_lane, b_lane):
    batch, n_heads = s0.shape[0], s0.shape[1]
    smem = pl.BlockSpec(memory_space=pltpu.SMEM)
    tile = lambda off: pl.BlockSpec((None, None, 1, DK_C), lambda b, h: (b, off + h, 0, 0))
    st = pl.BlockSpec((None, None, DK_C, DK_C), lambda b, h: (b, h, 0, 0))
    return pl.pallas_call(
        functools.partial(_sample_gdn_body, a_lane=a_lane, b_lane=b_lane),
        grid=(batch, n_heads),
        in_specs=[smem, smem, smem, tile(0), tile(n_heads), tile(2 * n_heads), tile(0), st,
                  pl.BlockSpec((1, DK_C), lambda b, h: (0, 0))],
        out_specs=[tile(0), st],
        out_shape=[jax.ShapeDtypeStruct((batch, n_heads, 1, DK_C), BF16), jax.ShapeDtypeStruct(s0.shape, F32)],
        compiler_params=_params("parallel", "parallel"),
        name="sample_gdn",
    )(us, a_log, dt_bias, xc, xc, xc, z, s0, norm_w.reshape(1, DK_C))


def kernel(x_prompt, x_sample, cache_k, cache_v, page_table, state_ssm, state_conv_ssm, state_delta, state_conv_delta, rel_bias_table, norm_w, w_in, lam_q1, lam_k1, lam_q2, lam_k2, subln_w, conv_w_b, conv_bias_b, dt_bias_b, a_log_b, d_skip_b, norm_b_w, conv_w_c, dt_bias_c, a_log_c, norm_c_w, w_branch_a, w_branch_b, w_branch_c, w_out, final_norm_w):
    batch, seq, d = x_prompt.shape
    dec_batch = x_sample.shape[0]
    depth = w_in.shape[0]
    hd = d // 2
    h_a = hd // DV_A
    h_b = hd // P_B
    h_c = hd // DK_C
    conv_b = hd + 2 * G_B * N_B
    conv_c = 3 * hd
    m = batch * seq
    ms = 16

    sizes = (hd, hd, hd, hd, hd, conv_b, h_b, conv_c, hd, h_c, h_c, 3 * d)
    offs = np.concatenate([[0], np.cumsum(sizes)]).tolist()
    o_dt, o_qkvc, o_ac, o_bc, o_gate = offs[6], offs[7], offs[9], offs[10], offs[11]
    c_xbc = 5 * hd
    c_qkvc = c_xbc + conv_b
    c_zc = c_qkvc + conv_c
    c_gate = c_zc + hd
    n_u = c_gate + 3 * d
    bn_u = _pick(math.gcd(n_u, hd), (512, 256, 128))
    regions = ((0, c_qkvc // bn_u, 0), (c_qkvc // bn_u, c_gate // bn_u, o_qkvc - c_qkvc),
               (c_gate // bn_u, n_u // bn_u, o_gate - c_gate))
    blk_dt, blk_ab = o_dt // LANES, o_ac // LANES
    a_lane, b_lane = o_ac % LANES, o_bc % LANES
    assert o_dt % LANES == 0 and h_b <= LANES and b_lane + h_c <= LANES
    assert c_qkvc % bn_u == 0 and c_gate % bn_u == 0 and o_gate - c_gate < LANES

    w_t = jnp.swapaxes(w_in, 1, 2)

    t_attn = min(ATTN_BLOCK, seq)
    bias = bias_tiles(rel_bias_table, t_attn)
    table2 = jnp.tile(rel_bias_table.T, (2, 1))

    xp = x_prompt.reshape(m, d)
    xs = jnp.zeros((ms, d), F32).at[:dec_batch].set(x_sample.reshape(dec_batch, d))
    k_all = jnp.zeros((depth, m, hd), F32)
    v_all = jnp.zeros((depth, m, hd), F32)
    outs = {k: [] for k in ("ks", "vs", "ssmp", "ssms", "cbp", "cbs", "dp", "ds", "ccp", "ccs")}

    for l in range(depth):
        lam_init = 0.8 - 0.6 * math.exp(-0.3 * l)
        lam = jnp.exp(jnp.sum(lam_q1[l] * lam_k1[l])) - jnp.exp(jnp.sum(lam_q2[l] * lam_k2[l])) + lam_init
        scal = jnp.stack([lam, jnp.asarray(1.0 - lam_init, F32)]).astype(F32)
        zero_bias_c = jnp.zeros((conv_c,), F32)

        hp = rmsnorm(xp, norm_w[l], BF16)
        u, k_all, v_all = input_projection(hp, w_t, l, hd, n_u, regions, BF16, k_all, v_all)
        us_dt, us_ab = small_logits(hp, w_t, l, blk_dt, blk_ab)
        u3 = u.reshape(batch, seq, -1)
        outs["cbp"].append(u3[:, seq - (CONV_K - 1):, c_xbc:c_xbc + conv_b].astype(F32))
        outs["ccp"].append(u3[:, seq - (CONV_K - 1):, c_qkvc:c_qkvc + conv_c].astype(F32))

        o_a = prompt_attention(u, bias, subln_w[l], scal, batch, seq, hd, t_attn)
        xc_b = prompt_conv(u, c_xbc, conv_b, conv_w_b[l], conv_bias_b[l], batch, seq)
        o_b, ssm = prompt_ssd(xc_b, us_dt, u, 4, dt_bias_b[l], a_log_b[l], d_skip_b[l], norm_b_w[l], batch, seq, hd)
        xc_c = prompt_conv(u, c_qkvc, conv_c, conv_w_c[l], zero_bias_c, batch, seq)
        o_c, delta = prompt_gdn(xc_c, us_ab, u, c_zc, a_log_c[l], dt_bias_c[l], norm_c_w[l],
                                batch, seq, hd, a_lane, b_lane)
        outs["ssmp"].append(ssm)
        outs["dp"].append(delta)
        merged = merge_branches(o_a, o_b, o_c, w_branch_a, w_branch_b, w_branch_c, l, u, c_gate)
        xp = out_projection(merged, w_out, l, xp)

        hs = rmsnorm(xs, norm_w[l], BF16)
        u_s = input_projection(hs, w_t, l, hd, n_u, regions, F32)
        us_dt_s, us_ab_s = small_logits(hs, w_t, l, blk_dt, blk_ab)
        ur = u_s[:dec_batch]
        q_s = ur[:, 0:hd].reshape(dec_batch, h_a, DV_A)
        k_s = ur[:, hd:2 * hd].reshape(dec_batch, h_a, DV_A)
        v_s = ur[:, 2 * hd:3 * hd].reshape(dec_batch, h_a, DV_A)
        z_s = ur[:, 3 * hd:4 * hd].reshape(dec_batch, h_a, DV_A)
        outs["ks"].append(k_s.reshape(dec_batch, 1, h_a, DV_A))
        outs["vs"].append(v_s.reshape(dec_batch, 1, h_a, DV_A))
        oa_s = sample_attention(q_s, k_s, v_s, z_s, cache_k, cache_v, l, page_table, table2, subln_w[l], scal)

        xcb_s, cb_state = sample_conv(state_conv_ssm[l], ur[:, c_xbc:c_xbc + conv_b], conv_w_b[l], conv_bias_b[l])
        ob_s, ssm_s = sample_ssd(xcb_s.reshape(dec_batch, 1, conv_b), us_dt_s[:dec_batch].reshape(dec_batch, 1, LANES),
                                 ur[:, 4 * hd:5 * hd].reshape(dec_batch, 1, hd), state_ssm[l],
                                 dt_bias_b[l], a_log_b[l], d_skip_b[l], norm_b_w[l])
        xcc_s, cc_state = sample_conv(state_conv_delta[l], ur[:, c_qkvc:c_qkvc + conv_c], conv_w_c[l], zero_bias_c)
        oc_s, delta_s = sample_gdn(xcc_s.reshape(dec_batch, 3 * h_c, 1, DK_C), us_ab_s,
                                   ur[:, c_zc:c_zc + hd].reshape(dec_batch, h_c, 1, DK_C), state_delta[l],
                                   a_log_c[l], dt_bias_c[l], norm_c_w[l], a_lane, b_lane)
        outs["cbs"].append(cb_state)
        outs["ccs"].append(cc_state)
        outs["ssms"].append(ssm_s)
        outs["ds"].append(delta_s)

        def pad_rows(t):
            return jnp.zeros((ms, hd), BF16).at[:dec_batch].set(t.reshape(dec_batch, hd))

        merged_s = merge_branches(pad_rows(oa_s), pad_rows(ob_s), pad_rows(oc_s),
                                  w_branch_a, w_branch_b, w_branch_c, l, u_s, c_gate)
        xs = out_projection(merged_s, w_out, l, xs)

    y_prompt = rmsnorm(xp, final_norm_w, F32).reshape(batch, seq, d)
    y_sample = rmsnorm(xs, final_norm_w, F32)[:dec_batch].reshape(dec_batch, 1, d)
    st = {k: jnp.stack(v, axis=0) for k, v in outs.items()}
    k_prompt = k_all.reshape(depth, batch, seq, h_a, DV_A)
    v_prompt = v_all.reshape(depth, batch, seq, h_a, DV_A)
    return (y_prompt, y_sample, k_prompt, v_prompt, st["ks"], st["vs"], st["ssmp"], st["ssms"],
            st["cbp"], st["cbs"], st["dp"], st["ds"], st["ccp"], st["ccs"])
```

```python
import functools
import math

import numpy as np
import jax
import jax.numpy as jnp
from jax import lax
from jax.experimental import pallas as pl
from jax.experimental.pallas import tpu as pltpu

F32 = jnp.float32
BF16 = jnp.bfloat16
HIGHEST = lax.Precision.HIGHEST

DK_A = 64
DV_A = 128
NUM_BUCKETS = 32
MAX_DISTANCE = 128
PAGE_SIZE = 128
P_B = 64
G_B = 4
N_B = 128
DK_C = 128
CONV_K = 4
EPS = 1e-6
NEG = -1e30
LOG2E = math.log2(math.e)

LANES = 128
VMEM_LIMIT = 56 * 1024 * 1024

ATTN_BLOCK = 512
SSD_CHUNK = 128
GDN_CHUNK = 64
GDN_SUPER = 256
GDN_HEADS = 4
DECODE_PAGES = 4


def _params(*sem):
    return pltpu.CompilerParams(dimension_semantics=sem, vmem_limit_bytes=VMEM_LIMIT)


def _pick(n, cands):
    for c in cands:
        if n % c == 0:
            return c
    return n


def _silu(x):
    return x * jax.nn.sigmoid(x)


def _softplus(x):
    return jnp.maximum(x, 0.0) + jnp.log1p(jnp.exp(-jnp.abs(x)))


def _dot(a, b, precision=None):
    return jnp.dot(a, b, preferred_element_type=F32, precision=precision)


def _dot_nt(a, b, precision=None):
    return lax.dot_general(a, b, (((1,), (1,)), ((), ())), preferred_element_type=F32, precision=precision)


def _dot_tn(a, b, precision=None):
    return lax.dot_general(a, b, (((0,), (0,)), ((), ())), preferred_element_type=F32, precision=precision)


def _rmsnorm_body(x_ref, w_ref, o_ref):
    x = x_ref[...]
    r = lax.rsqrt(jnp.mean(x * x, axis=-1, keepdims=True) + EPS)
    o_ref[...] = ((x * r) * w_ref[...]).astype(o_ref.dtype)


def rmsnorm(x, w, out_dtype):
    m, d = x.shape
    bm = _pick(m, (256, 128, 64, 32, 16, 8))
    return pl.pallas_call(
        _rmsnorm_body,
        grid=(m // bm,),
        in_specs=[pl.BlockSpec((bm, d), lambda i: (i, 0)), pl.BlockSpec((1, d), lambda i: (0, 0))],
        out_specs=pl.BlockSpec((bm, d), lambda i: (i, 0)),
        out_shape=jax.ShapeDtypeStruct((m, d), out_dtype),
        compiler_params=_params("parallel"),
        name="rmsnorm",
    )(x, w.reshape(1, d))


def _stage_weights(w_ref, e_ref, dst_ref, shift):
    n = dst_ref.shape[0]
    if shift == 0:
        dst_ref[...] = w_ref[...].astype(BF16)
    else:
        dst_ref[0:n - shift, :] = w_ref[shift:n, :].astype(BF16)
        dst_ref[n - shift:n, :] = e_ref[0:shift, :].astype(BF16)


def _inproj_body(x_ref, xs_ref, w_ref, e_ref, k_in, v_in, o_ref, os_ref, k_ref, v_ref, wb_ref, *, tiles, regions):
    del k_in, v_in
    j = pl.program_id(0)
    i = pl.program_id(1)

    @pl.when(i == 0)
    def _():
        for lo, hi, shift in regions:
            @pl.when((j >= lo) & (j < hi))
            def _():
                _stage_weights(w_ref, e_ref, wb_ref, shift)
        os_ref[...] = _dot_nt(xs_ref[...], wb_ref[...])

    acc = _dot_nt(x_ref[...], wb_ref[...])
    o_ref[...] = acc.astype(o_ref.dtype)

    @pl.when((j >= tiles) & (j < 2 * tiles))
    def _():
        k_ref[...] = acc

    @pl.when((j >= 2 * tiles) & (j < 3 * tiles))
    def _():
        v_ref[...] = acc


def input_projection(x, xs, w_t, layer, hd, n_out, regions, k_all, v_all):
    m, k = x.shape
    ms = xs.shape[0]
    bm = _pick(m, (1024, 512, 256, 128, 64, 32, 16))
    bn = _pick(math.gcd(n_out, hd), (512, 256, 128))
    tiles = hd // bn
    nj, ni = n_out // bn, m // bm
    any_spec = pl.BlockSpec(memory_space=pl.ANY)

    def slab(first):
        def index(j, i):
            row = jnp.where(j < first, 0, jnp.where(j >= first + tiles, ni - 1, i))
            return (layer, row, jnp.clip(j - first, 0, tiles - 1))
        return pl.BlockSpec((None, bm, bn), index)

    return pl.pallas_call(
        functools.partial(_inproj_body, tiles=tiles, regions=regions),
        grid=(nj, ni),
        in_specs=[pl.BlockSpec((bm, k), lambda j, i: (i, 0)),
                  pl.BlockSpec((ms, k), lambda j, i: (0, 0)),
                  pl.BlockSpec((None, bn, k), lambda j, i: (layer, j, 0)),
                  pl.BlockSpec((None, LANES, k), lambda j, i: (layer, (j + 1) * (bn // LANES), 0)),
                  any_spec, any_spec],
        out_specs=[pl.BlockSpec((bm, bn), lambda j, i: (i, j)),
                   pl.BlockSpec((ms, bn), lambda j, i: (0, j)),
                   slab(tiles), slab(2 * tiles)],
        out_shape=[jax.ShapeDtypeStruct((m, n_out), BF16), jax.ShapeDtypeStruct((ms, n_out), F32),
                   jax.ShapeDtypeStruct(k_all.shape, F32), jax.ShapeDtypeStruct(v_all.shape, F32)],
        scratch_shapes=[pltpu.VMEM((bn, k), BF16)],
        input_output_aliases={4: 2, 5: 3},
        compiler_params=_params("arbitrary", "arbitrary"),
        name="inproj",
    )(x, xs, w_t, w_t, k_all, v_all)


def _small_logits_body(x_ref, wa_ref, wb_ref, oa_ref, ob_ref):
    x = x_ref[...]
    oa_ref[...] = _dot_nt(x, wa_ref[...].astype(BF16))
    ob_ref[...] = _dot_nt(x, wb_ref[...].astype(BF16))


def small_logits(x, w_t, layer, blk_a, blk_b):
    m, k = x.shape
    bm = _pick(m, (1024, 512, 256, 128, 64, 32, 16))
    wspec = lambda blk: pl.BlockSpec((None, LANES, k), lambda i: (layer, blk, 0))
    ospec = pl.BlockSpec((bm, LANES), lambda i: (i, 0))
    return pl.pallas_call(
        _small_logits_body,
        grid=(m // bm,),
        in_specs=[pl.BlockSpec((bm, k), lambda i: (i, 0)), wspec(blk_a), wspec(blk_b)],
        out_specs=[ospec, ospec],
        out_shape=[jax.ShapeDtypeStruct((m, LANES), F32)] * 2,
        compiler_params=_params("parallel"),
        name="small_logits",
    )(x, w_t, w_t)


def _merge_body(oa_ref, ob_ref, oc_ref, wa_ref, wb_ref, wc_ref, ga_ref, gb_ref, gc_ref, o_ref, sa, sb, sc):
    @pl.when(pl.program_id(1) == 0)
    def _():
        sa[...] = wa_ref[...].astype(BF16)
        sb[...] = wb_ref[...].astype(BF16)
        sc[...] = wc_ref[...].astype(BF16)

    acc = jax.nn.sigmoid(ga_ref[...].astype(F32)) * _dot(oa_ref[...], sa[...])
    acc = acc + jax.nn.sigmoid(gb_ref[...].astype(F32)) * _dot(ob_ref[...], sb[...])
    acc = acc + jax.nn.sigmoid(gc_ref[...].astype(F32)) * _dot(oc_ref[...], sc[...])
    o_ref[...] = acc.astype(o_ref.dtype)


def merge_branches(o_a, o_b, o_c, w_a, w_b, w_c, layer, u, gate_off):
    m, hd = o_a.shape
    d = w_a.shape[2]
    bm = _pick(m, (512, 256, 128, 64, 32, 16))
    bn = _pick(math.gcd(d, gate_off), (512, 256, 128))
    g0 = gate_off // bn
    nd = d // bn
    row = pl.BlockSpec((bm, hd), lambda j, i: (i, 0))
    wsp = pl.BlockSpec((None, hd, bn), lambda j, i: (layer, 0, j))

    def gate(t):
        return pl.BlockSpec((bm, bn), lambda j, i: (i, g0 + t * nd + j))

    return pl.pallas_call(
        _merge_body,
        grid=(nd, m // bm),
        in_specs=[row, row, row, wsp, wsp, wsp, gate(0), gate(1), gate(2)],
        out_specs=pl.BlockSpec((bm, bn), lambda j, i: (i, j)),
        out_shape=jax.ShapeDtypeStruct((m, d), BF16),
        scratch_shapes=[pltpu.VMEM((hd, bn), BF16)] * 3,
        compiler_params=_params("parallel", "arbitrary"),
        name="merge",
    )(o_a, o_b, o_c, w_a, w_b, w_c, u, u, u)


def _outproj_body(m_ref, w_ref, x_ref, o_ref, ws):
    @pl.when(pl.program_id(1) == 0)
    def _():
        ws[...] = w_ref[...].astype(BF16)

    o_ref[...] = x_ref[...] + _dot(m_ref[...], ws[...])


def out_projection(merged, w_out, layer, x):
    m, d = merged.shape
    bm = _pick(m, (1024, 512, 256, 128, 64, 32, 16))
    bn = _pick(d, (512, 256, 128))
    return pl.pallas_call(
        _outproj_body,
        grid=(d // bn, m // bm),
        in_specs=[pl.BlockSpec((bm, d), lambda j, i: (i, 0)), pl.BlockSpec((None, d, bn), lambda j, i: (layer, 0, j)),
                  pl.BlockSpec((bm, bn), lambda j, i: (i, j))],
        out_specs=pl.BlockSpec((bm, bn), lambda j, i: (i, j)),
        out_shape=jax.ShapeDtypeStruct((m, d), F32),
        scratch_shapes=[pltpu.VMEM((d, bn), BF16)],
        compiler_params=_params("parallel", "arbitrary"),
        name="outproj",
    )(merged, w_out, x)


def _bucket_changes():
    max_exact = NUM_BUCKETS // 2
    n = np.arange(0, MAX_DISTANCE + 1)
    nf = np.maximum(n, 1).astype(np.float32)
    large = max_exact + (np.log(nf / np.float32(max_exact)) / np.float32(math.log(MAX_DISTANCE / max_exact))
                         * np.float32(NUM_BUCKETS - max_exact)).astype(np.int32)
    bucket = np.where(n < max_exact, n, np.minimum(large, NUM_BUCKETS - 1))
    bucket[MAX_DISTANCE] = NUM_BUCKETS - 1
    changes = [(0, int(bucket[0]))]
    for d in range(1, MAX_DISTANCE + 1):
        if bucket[d] != bucket[d - 1]:
            changes.append((d, int(bucket[d])))
    return changes


def _bias_tiles_body(tbl_ref, o_ref, *, t, n_heads):
    h = pl.program_id(0)
    i = lax.broadcasted_iota(jnp.int32, (t, t), 0)
    j = lax.broadcasted_iota(jnp.int32, (t, t), 1)
    last = tbl_ref[(NUM_BUCKETS - 1) * n_heads + h]
    changes = _bucket_changes()
    for tile, off in ((0, 0), (1, t)):
        d = i - j + off
        val = jnp.full((t, t), (tbl_ref[changes[0][1] * n_heads + h] - last) * LOG2E, F32)
        for ds, b in changes[1:]:
            val = jnp.where(d >= ds, (tbl_ref[b * n_heads + h] - last) * LOG2E, val)
        if off == 0:
            val = jnp.where(j <= i, val, NEG)
        o_ref[tile] = val


def bias_tiles(table, t):
    n_heads = table.shape[1]
    return pl.pallas_call(
        functools.partial(_bias_tiles_body, t=t, n_heads=n_heads),
        grid=(n_heads,),
        in_specs=[pl.BlockSpec(memory_space=pltpu.SMEM)],
        out_specs=pl.BlockSpec((None, 2, t, t), lambda h: (h, 0, 0, 0)),
        out_shape=jax.ShapeDtypeStruct((n_heads, 2, t, t), F32),
        compiler_params=_params("parallel"),
        name="bias_tiles",
    )(table.reshape(-1))


def _attn_body(scal_ref, q_ref, kb_ref, vb_ref, z_ref, bias_ref, sw_ref, o_ref, *, t):
    qi = pl.program_id(2)
    q = q_ref[...].astype(F32) * (DK_A ** -0.5 * LOG2E)
    lane = lax.broadcasted_iota(jnp.int32, q.shape, 1)
    qq = jnp.concatenate([jnp.where(lane < DK_A, q, 0.0), jnp.where(lane >= DK_A, q, 0.0)], axis=0).astype(BF16)

    def step(kc, carry, bias_idx):
        m, l, acc = carry
        start = pl.multiple_of(kc * t, t)
        s = _dot_nt(qq, kb_ref[pl.ds(start, t), :])
        if bias_idx is not None:
            s = (s.reshape(2, t, t) + bias_ref[bias_idx][None]).reshape(2 * t, t)
        m_new = jnp.maximum(m, jnp.max(s, axis=-1, keepdims=True))
        alpha = jnp.exp2(m - m_new)
        p = jnp.exp2(s - m_new)
        l = alpha * l + jnp.sum(p, axis=-1, keepdims=True)
        acc = alpha * acc + _dot(p.astype(BF16), vb_ref[pl.ds(start, t), :])
        return m_new, l, acc

    carry = (jnp.full((2 * t, 1), NEG, F32), jnp.zeros((2 * t, 1), F32), jnp.zeros((2 * t, DV_A), F32))
    far = jnp.maximum(qi - 1, 0)
    carry = lax.fori_loop(0, far, lambda kc, c: step(kc, c, None), carry)
    carry = lax.fori_loop(far, qi, lambda kc, c: step(kc, c, 1), carry)
    _, l, acc = step(qi, carry, 0)

    o = acc / l
    o = o[:t] - scal_ref[0] * o[t:]
    r = lax.rsqrt(jnp.mean(o * o, axis=-1, keepdims=True) + EPS)
    o = (o * r) * sw_ref[...] * scal_ref[1]
    o_ref[...] = (o * _silu(z_ref[...].astype(F32))).astype(o_ref.dtype)


def prompt_attention(u, bias, subln_w, scal, batch, seq, hd, t):
    n_heads = hd // DV_A
    nq = seq // t
    return pl.pallas_call(
        functools.partial(_attn_body, t=t),
        grid=(batch, n_heads, nq),
        in_specs=[
            pl.BlockSpec(memory_space=pltpu.SMEM),
            pl.BlockSpec((t, DV_A), lambda b, h, qi: (b * nq + qi, h)),
            pl.BlockSpec((seq, DV_A), lambda b, h, qi: (b, n_heads + h)),
            pl.BlockSpec((seq, DV_A), lambda b, h, qi: (b, 2 * n_heads + h)),
            pl.BlockSpec((t, DV_A), lambda b, h, qi: (b * nq + qi, 3 * n_heads + h)),
            pl.BlockSpec((None, 2, t, t), lambda b, h, qi: (h, 0, 0, 0)),
            pl.BlockSpec((1, DV_A), lambda b, h, qi: (0, 0)),
        ],
        out_specs=pl.BlockSpec((t, DV_A), lambda b, h, qi: (b * nq + qi, h)),
        out_shape=jax.ShapeDtypeStruct((batch * seq, hd), BF16),
        compiler_params=_params("parallel", "parallel", "arbitrary"),
        name="prompt_attn",
    )(scal, u, u, u, u, bias, subln_w.reshape(1, DV_A))


def _conv_body(x_ref, w_ref, b_ref, o_ref):
    x = x_ref[...].astype(F32)
    row = lax.broadcasted_iota(jnp.int32, x.shape, 0)
    acc = x * w_ref[CONV_K - 1:CONV_K, :]
    for k in range(1, CONV_K):
        shifted = jnp.where(row >= k, pltpu.roll(x, k, axis=0), 0.0)
        acc = acc + shifted * w_ref[CONV_K - 1 - k:CONV_K - k, :]
    acc = acc + b_ref[...]
    o_ref[...] = _silu(acc)


def prompt_conv(u, col_off, width, w, bias, batch, seq):
    cw = _pick(math.gcd(col_off, width), (512, 256, 128))
    c0 = col_off // cw
    return pl.pallas_call(
        _conv_body,
        grid=(batch, width // cw),
        in_specs=[pl.BlockSpec((seq, cw), lambda b, c: (b, c0 + c)),
                  pl.BlockSpec((CONV_K, cw), lambda b, c: (0, c)),
                  pl.BlockSpec((1, cw), lambda b, c: (0, c))],
        out_specs=pl.BlockSpec((seq, cw), lambda b, c: (b, c)),
        out_shape=jax.ShapeDtypeStruct((batch * seq, width), F32),
        compiler_params=_params("parallel", "parallel"),
        name="prompt_conv",
    )(u, w, bias.reshape(1, width))


def _sample_conv_body(prev_ref, x_ref, w_ref, b_ref, y_ref, st_ref):
    x = x_ref[...]
    acc = x * w_ref[CONV_K - 1:CONV_K, :]
    for i in range(CONV_K - 1):
        acc = acc + prev_ref[i] * w_ref[i:i + 1, :]
    y_ref[...] = _silu(acc + b_ref[...])
    for i in range(CONV_K - 2):
        st_ref[i] = prev_ref[i + 1]
    st_ref[CONV_K - 2] = x


def sample_conv(prev, x, w, bias):
    rows, c = x.shape
    prev_t = jnp.transpose(prev, (1, 0, 2))
    cw = _pick(c, (1024, 512, 256, 128))
    y, st = pl.pallas_call(
        _sample_conv_body,
        grid=(c // cw,),
        in_specs=[pl.BlockSpec((CONV_K - 1, rows, cw), lambda j: (0, 0, j)),
                  pl.BlockSpec((rows, cw), lambda j: (0, j)),
                  pl.BlockSpec((CONV_K, cw), lambda j: (0, j)),
                  pl.BlockSpec((1, cw), lambda j: (0, j))],
        out_specs=[pl.BlockSpec((rows, cw), lambda j: (0, j)),
                   pl.BlockSpec((CONV_K - 1, rows, cw), lambda j: (0, 0, j))],
        out_shape=[jax.ShapeDtypeStruct((rows, c), F32), jax.ShapeDtypeStruct((CONV_K - 1, rows, c), F32)],
        compiler_params=_params("parallel"),
        name="sample_conv",
    )(prev_t, x, w, bias.reshape(1, c))
    return y, jnp.transpose(st, (1, 0, 2))


def _ssd_body(x_ref, us_ref, z_ref, dtb_ref, alog_ref, dskip_ref, nw_ref, o_ref, h_ref, *, t, hd, n_heads):
    c = pl.program_id(1)

    @pl.when(c == 0)
    def _():
        h_ref[...] = jnp.zeros_like(h_ref)

    heads_per_group = n_heads // G_B
    dt = _softplus(us_ref[:, 0:n_heads] + dtb_ref[...])
    a = dt * (-jnp.exp(alog_ref[...]))
    ri = lax.broadcasted_iota(jnp.int32, (t, t), 0)
    ci = lax.broadcasted_iota(jnp.int32, (t, t), 1)
    lower = ci <= ri
    a_col = _dot(lower.astype(F32), a, HIGHEST)
    a_row = _dot_tn(a, (ri <= ci).astype(F32), HIGHEST)
    a_last = a_col[t - 1:t, :]
    e_col = jnp.exp(a_col)
    e_end = jnp.exp(a_last - a_col)
    e_last = jnp.exp(a_last)
    lane = lax.broadcasted_iota(jnp.int32, (t, LANES), 1)
    lo = lane < P_B
    row_lo = lax.broadcasted_iota(jnp.int32, (2 * P_B, N_B), 0) < P_B
    dskip = dskip_ref[...]

    cb = {}
    ys = []
    for j in range(n_heads // 2):
        ha, hb = 2 * j, 2 * j + 1
        g = ha // heads_per_group
        bg = x_ref[:, hd + g * N_B:hd + (g + 1) * N_B].astype(BF16)
        cg = x_ref[:, hd + (G_B + g) * N_B:hd + (G_B + g + 1) * N_B].astype(BF16)
        if g not in cb:
            cb[g] = _dot_nt(cg, bg)
        x = x_ref[:, j * LANES:(j + 1) * LANES]
        xdt = x * jnp.where(lo, dt[:, ha:ha + 1], dt[:, hb:hb + 1])
        xdt_b = xdt.astype(BF16)
        y = None
        for hh, keep in ((ha, lo), (hb, ~lo)):
            seg = a_col[:, hh:hh + 1] - a_row[hh:hh + 1, :]
            w = (cb[g] * jnp.exp(jnp.where(lower, seg, NEG))).astype(BF16)
            part = _dot(w, jnp.where(keep, xdt_b, jnp.zeros_like(xdt_b)))
            y = part if y is None else y + part
        h_prev = h_ref[ha:hb + 1].reshape(2 * P_B, N_B)
        y = y + _dot_nt(cg, h_prev.astype(BF16)) * jnp.where(lo, e_col[:, ha:ha + 1], e_col[:, hb:hb + 1])
        xdec = (xdt * jnp.where(lo, e_end[:, ha:ha + 1], e_end[:, hb:hb + 1])).astype(BF16)
        st = _dot_tn(xdec, bg)
        h_new = h_prev * jnp.where(row_lo, e_last[:, ha:ha + 1], e_last[:, hb:hb + 1]) + st
        h_ref[ha:hb + 1] = h_new.reshape(2, P_B, N_B)
        y = y + x * jnp.where(lo[:1], dskip[:, ha:ha + 1], dskip[:, hb:hb + 1])
        ys.append(y)
    y = jnp.concatenate(ys, axis=1)
    y = y * _silu(z_ref[...].astype(F32))
    r = lax.rsqrt(jnp.mean(y * y, axis=-1, keepdims=True) + EPS)
    o_ref[...] = ((y * r) * nw_ref[...]).astype(o_ref.dtype)


def prompt_ssd(xc, us, u, z_blk, dt_bias, a_log, d_skip, norm_w, batch, seq, hd):
    n_heads = hd // P_B
    t = min(SSD_CHUNK, seq)
    nc = seq // t
    cdim = xc.shape[1]
    vec = pl.BlockSpec((1, n_heads), lambda b, c: (0, 0))
    return pl.pallas_call(
        functools.partial(_ssd_body, t=t, hd=hd, n_heads=n_heads),
        grid=(batch, nc),
        in_specs=[pl.BlockSpec((t, cdim), lambda b, c: (b * nc + c, 0)),
                  pl.BlockSpec((t, LANES), lambda b, c: (b * nc + c, 0)),
                  pl.BlockSpec((t, hd), lambda b, c: (b * nc + c, z_blk)),
                  vec, vec, vec,
                  pl.BlockSpec((1, hd), lambda b, c: (0, 0))],
        out_specs=[pl.BlockSpec((t, hd), lambda b, c: (b * nc + c, 0)),
                   pl.BlockSpec((None, n_heads, P_B, N_B), lambda b, c: (b, 0, 0, 0))],
        out_shape=[jax.ShapeDtypeStruct((batch * seq, hd), BF16),
                   jax.ShapeDtypeStruct((batch, n_heads, P_B, N_B), F32)],
        compiler_params=_params("parallel", "arbitrary"),
        name="prompt_ssd",
    )(xc, us, u, dt_bias.reshape(1, -1), a_log.reshape(1, -1), d_skip.reshape(1, -1), norm_w.reshape(1, hd))


def _l2norm(x):
    return x * lax.rsqrt(jnp.sum(x * x, axis=-1, keepdims=True) + EPS)


def _split3(x):
    a = x.astype(BF16).astype(F32)
    r = x - a
    b = r.astype(BF16).astype(F32)
    return a, b, (r - b).astype(BF16).astype(F32)


def _gdn_body(alog_ref, dtb_ref, q_ref, k_ref, v_ref, z_ref, us_ref, nw_ref, o_ref, s_ref,
              u_s, wq_s, qk_s, kd_s, gl_s, *, tc, sb, nc, a_lane, b_lane):
    pid = pl.program_id(1)
    cps = sb // tc
    shift = tc.bit_length() - 1
    ri = lax.broadcasted_iota(jnp.int32, (sb, sb), 0)
    ci = lax.broadcasted_iota(jnp.int32, (sb, sb), 1)
    same = lax.shift_right_logical(ri, shift) == lax.shift_right_logical(ci, shift)
    incl = same & (ci <= ri)
    strict = same & (ci < ri)
    eye = jnp.where(ri == ci, 1.0, 0.0)
    tri = jnp.where(incl, 1.0, 0.0).astype(BF16)
    tri_t = jnp.where(same & (ri <= ci), 1.0, 0.0).astype(BF16)
    tri_ones = jnp.concatenate([tri, jnp.where(same, 1.0, 0.0).astype(BF16)], axis=0)
    lane = lax.broadcasted_iota(jnp.int32, (sb, LANES), 1)

    def sum3(x):
        return x[:, 0:1] + x[:, 1:2] + x[:, 2:3]

    heads = range(GDN_HEADS)

    def each(f, *lists):
        return [f(*args) for args in zip(*lists)]

    def prepare_all(sidx, carry):
        rows = pl.ds(pl.multiple_of(sidx * sb, sb), sb)
        us = us_ref[rows, :]
        hs = [pid * GDN_HEADS + hh for hh in heads]
        cols = [slice(hh * DK_C, (hh + 1) * DK_C) for hh in heads]
        a_col = each(lambda h: jnp.sum(jnp.where(lane == a_lane + h, us, 0.0), axis=-1, keepdims=True), hs)
        b_col = each(lambda h: jnp.sum(jnp.where(lane == b_lane + h, us, 0.0), axis=-1, keepdims=True), hs)
        g = each(lambda h, a: -jnp.exp(jnp.full((1, 1), alog_ref[h], F32)) * _softplus(a + dtb_ref[h]), hs, a_col)
        beta = each(jax.nn.sigmoid, b_col)
        parts = each(_split3, g)
        gm = each(lambda t: jnp.where(lane == 0, t[0], jnp.where(lane == 1, t[1], jnp.where(lane == 2, t[2], 0.0)))
                  .astype(BF16), parts)
        cum = each(lambda x: _dot(tri_ones, x), gm)
        cum_t = each(lambda x: _dot_tn(x, tri_t), gm)
        gc = each(lambda x: sum3(x[:sb]), cum)
        g_end = each(lambda x: sum3(x[sb:]), cum)
        gc_row = each(lambda x: x[0:1] + x[1:2] + x[2:3], cum_t)
        decay = each(lambda c, r: jnp.exp(jnp.where(incl, c - r, NEG)), gc, gc_row)
        q = each(lambda c: _l2norm(q_ref[rows, c]) * (DK_C ** -0.5), cols)
        k = each(lambda c: _l2norm(k_ref[rows, c]), cols)
        kb = each(lambda x, b: x * b, k, beta)
        k_b = each(lambda x: x.astype(BF16), k)
        kq = each(lambda a, b, c: _dot_nt(jnp.concatenate([a, b], axis=0).astype(BF16), c), kb, q, k_b)
        m = each(lambda x, d: jnp.where(strict, x[:sb] * d, 0.0), kq, decay)
        x = each(lambda a: eye - a, m)
        p = each(lambda a: a.astype(BF16), m)
        n = 2
        while n <= tc // 2:
            p = each(lambda a: _dot(a, a).astype(BF16), p)
            x = each(lambda a, b: a + _dot(a.astype(BF16), b), x, p)
            n *= 2
        a_mat = each(lambda a: eye + a, m)
        a_hi = each(lambda a: a.astype(BF16), a_mat)
        a_lo = each(lambda a, b: (a - b.astype(F32)).astype(BF16), a_mat, a_hi)
        x_hi = each(lambda a: a.astype(BF16), x)
        x_lo = each(lambda a, b: (a - b.astype(F32)).astype(BF16), x, x_hi)
        y = each(lambda a, b, c: _dot(jnp.concatenate([a, b], axis=0), c), a_hi, a_lo, x_hi)
        y2 = each(_dot, a_hi, x_lo)
        resid = each(lambda a, b: eye - (a[:sb] + a[sb:] + b), y, y2)
        t_inv = each(lambda a, b, c: (a + _dot(b, c.astype(BF16))).astype(BF16), x, x_hi, resid)
        e_gc = each(jnp.exp, gc)
        rhs = each(lambda c, b, a, e: jnp.concatenate([v_ref[rows, c] * b, a * e], axis=1).astype(BF16),
                   cols, beta, kb, e_gc)
        uw = each(_dot, t_inv, rhs)
        qk = each(lambda a, d: (a[sb:] * d).astype(BF16), kq, decay)
        for hh in heads:
            u_s[hh, rows, :] = uw[hh][:, :DK_C]
            w = uw[hh][:, DK_C:].astype(BF16)
            qd = (q[hh] * e_gc[hh]).astype(BF16)
            kd_s[hh, rows, :] = (k[hh] * jnp.exp(g_end[hh] - gc[hh])).astype(BF16)
            e_end = jnp.exp(g_end[hh])
            for c in range(cps):
                cidx = sidx * cps + c
                r0 = c * tc
                wq_s[hh, cidx, 0:tc, :] = w[r0:r0 + tc]
                wq_s[hh, cidx, tc:2 * tc, :] = qd[r0:r0 + tc]
                qk_s[hh, cidx] = qk[hh][r0:r0 + tc, r0:r0 + tc]
                gl_s[hh, pl.ds(cidx, 1), :] = jnp.broadcast_to(e_end[r0:r0 + 1, :], (1, LANES))
        return carry

    lax.fori_loop(0, nc // cps, prepare_all, 0)

    def recur(cidx, states):
        rows = pl.ds(pl.multiple_of(cidx * tc, tc), tc)
        new = []
        for hh in range(GDN_HEADS):
            cols = slice(hh * DK_C, (hh + 1) * DK_C)
            s = states[hh]
            s_b = s.astype(BF16)
            ws = _dot(wq_s[hh, cidx], s_b)
            v_b = (u_s[hh, rows, :] - ws[:tc]).astype(BF16)
            o = ws[tc:] + _dot(qk_s[hh, cidx], v_b)
            r = lax.rsqrt(jnp.mean(o * o, axis=-1, keepdims=True) + EPS)
            o_ref[rows, cols] = ((o * r) * nw_ref[...] * _silu(z_ref[rows, cols].astype(F32))).astype(o_ref.dtype)
            new.append(s * gl_s[hh, pl.ds(cidx, 1), :] + _dot_tn(kd_s[hh, rows, :], v_b))
        return tuple(new)

    final = lax.fori_loop(0, nc, recur, tuple(jnp.zeros((DK_C, DK_C), F32) for _ in range(GDN_HEADS)))
    for hh in range(GDN_HEADS):
        s_ref[hh] = final[hh]


def prompt_gdn(xc, us, u, z_col, a_log, dt_bias, norm_w, batch, seq, hd, a_lane, b_lane):
    n_heads = hd // DK_C
    tc = min(GDN_CHUNK, seq)
    sb = min(GDN_SUPER, seq)
    nc = seq // tc
    gw = GDN_HEADS * DK_C
    smem = pl.BlockSpec(memory_space=pltpu.SMEM)
    heads = lambda col: pl.BlockSpec((seq, gw), lambda b, h: (b, col // gw + h))
    return pl.pallas_call(
        functools.partial(_gdn_body, tc=tc, sb=sb, nc=nc, a_lane=a_lane, b_lane=b_lane),
        grid=(batch, n_heads // GDN_HEADS),
        in_specs=[smem, smem, heads(0), heads(hd), heads(2 * hd), heads(z_col),
                  pl.BlockSpec((seq, LANES), lambda b, h: (b, 0)),
                  pl.BlockSpec((1, DK_C), lambda b, h: (0, 0))],
        out_specs=[pl.BlockSpec((seq, gw), lambda b, h: (b, h)),
                   pl.BlockSpec((None, GDN_HEADS, DK_C, DK_C), lambda b, h: (b, h, 0, 0))],
        out_shape=[jax.ShapeDtypeStruct((batch * seq, hd), BF16),
                   jax.ShapeDtypeStruct((batch, n_heads, DK_C, DK_C), F32)],
        scratch_shapes=[pltpu.VMEM((GDN_HEADS, seq, DK_C), F32),
                        pltpu.VMEM((GDN_HEADS, nc, 2 * tc, DK_C), BF16),
                        pltpu.VMEM((GDN_HEADS, nc, tc, tc), BF16),
                        pltpu.VMEM((GDN_HEADS, seq, DK_C), BF16),
                        pltpu.VMEM((GDN_HEADS, nc, LANES), F32)],
        compiler_params=_params("parallel", "parallel"),
        name="prompt_gdn",
    )(a_log, dt_bias, xc, xc, xc, u, us, norm_w.reshape(1, DK_C))


def _sample_attn_body(pt_ref, scal_ref, q_ref, kn_ref, vn_ref, z_ref, tbl_ref, sw_ref, *refs,
                      n_heads, n_steps, pages):
    k_refs, v_refs = refs[:pages], refs[pages:2 * pages]
    o_ref, m_s, l_s, acc_s = refs[2 * pages:]
    p = pl.program_id(1)
    rows = 2 * n_heads
    cols = PAGE_SIZE * n_heads

    @pl.when(p == 0)
    def _():
        m_s[...] = jnp.full_like(m_s, NEG)
        l_s[...] = jnp.zeros_like(l_s)
        acc_s[...] = jnp.zeros_like(acc_s)

    q = q_ref[...] * (DK_A ** -0.5)
    lane = lax.broadcasted_iota(jnp.int32, q.shape, 1)
    qq = jnp.concatenate([jnp.where(lane < DK_A, q, 0.0), jnp.where(lane >= DK_A, q, 0.0)], axis=0).astype(BF16)
    r_i = lax.broadcasted_iota(jnp.int32, (rows, cols), 0)
    c_i = lax.broadcasted_iota(jnp.int32, (rows, cols), 1)
    own = (c_i % n_heads) == (r_i % n_heads)
    last = tbl_ref[:, NUM_BUCKETS - 1:NUM_BUCKETS]
    changes = _bucket_changes()
    delta0 = tbl_ref[:, changes[0][1]:changes[0][1] + 1] - last

    def add_bias(s):
        d = PAGE_SIZE - c_i // n_heads
        val = jnp.broadcast_to(delta0, (rows, cols))
        for ds, b in changes[1:]:
            val = jnp.where(d >= ds, tbl_ref[:, b:b + 1] - last, val)
        return s + val

    ss = [_dot_nt(qq, k_refs[i][...].reshape(cols, DV_A).astype(BF16)) for i in range(pages)]
    ss[-1] = lax.cond(p == n_steps - 1, add_bias, lambda s: s, ss[-1])
    ss = [jnp.where(own, s, NEG) for s in ss]
    m_new = m_s[...]
    for s in ss:
        m_new = jnp.maximum(m_new, jnp.max(s, axis=-1, keepdims=True))
    alpha = jnp.exp(m_s[...] - m_new)
    l_new = alpha * l_s[...]
    acc = alpha * acc_s[...]
    for i, s in enumerate(ss):
        pr = jnp.where(own, jnp.exp(s - m_new), 0.0)
        l_new = l_new + jnp.sum(pr, axis=-1, keepdims=True)
        acc = acc + _dot(pr.astype(BF16), v_refs[i][...].reshape(cols, DV_A).astype(BF16))
    m_s[...] = m_new
    l_s[...] = l_new
    acc_s[...] = acc

    @pl.when(p == n_steps - 1)
    def _():
        kn = jnp.concatenate([kn_ref[...], kn_ref[...]], axis=0).astype(BF16).astype(F32)
        vn = jnp.concatenate([vn_ref[...], vn_ref[...]], axis=0).astype(BF16).astype(F32)
        s_new = jnp.sum(qq.astype(F32) * kn, axis=-1, keepdims=True) + delta0
        m_fin = jnp.maximum(m_new, s_new)
        a2 = jnp.exp(m_new - m_fin)
        p_new = jnp.exp(s_new - m_fin)
        l = a2 * l_new + p_new
        o = (a2 * acc + p_new.astype(BF16).astype(F32) * vn) / l
        o = o[:n_heads] - scal_ref[0] * o[n_heads:]
        r = lax.rsqrt(jnp.mean(o * o, axis=-1, keepdims=True) + EPS)
        o = (o * r) * sw_ref[...] * scal_ref[1]
        o_ref[...] = (o * _silu(z_ref[...])).astype(o_ref.dtype)


def sample_attention(q, k_new, v_new, z, cache_k, cache_v, layer, page_table, table2, subln_w, scal):
    batch, n_heads, _ = q.shape
    n_pages = page_table.shape[1]
    pages = _pick(n_pages, (DECODE_PAGES, 2, 1))
    n_steps = n_pages // pages
    per_seq = pl.BlockSpec((None, n_heads, DV_A), lambda b, p, pt: (b, 0, 0))

    def page(i):
        return pl.BlockSpec((None, None, PAGE_SIZE, n_heads, DV_A),
                            lambda b, p, pt: (layer, pt[b * n_pages + p * pages + i], 0, 0, 0))

    grid_spec = pltpu.PrefetchScalarGridSpec(
        num_scalar_prefetch=1,
        grid=(batch, n_steps),
        in_specs=[pl.BlockSpec(memory_space=pltpu.SMEM), per_seq, per_seq, per_seq, per_seq,
                  pl.BlockSpec((2 * n_heads, NUM_BUCKETS), lambda b, p, pt: (0, 0)),
                  pl.BlockSpec((1, DV_A), lambda b, p, pt: (0, 0))]
                 + [page(i) for i in range(pages)] + [page(i) for i in range(pages)],
        out_specs=per_seq,
        scratch_shapes=[pltpu.VMEM((2 * n_heads, 1), F32), pltpu.VMEM((2 * n_heads, 1), F32),
                        pltpu.VMEM((2 * n_heads, DV_A), F32)],
    )
    return pl.pallas_call(
        functools.partial(_sample_attn_body, n_heads=n_heads, n_steps=n_steps, pages=pages),
        grid_spec=grid_spec,
        out_shape=jax.ShapeDtypeStruct((batch, n_heads, DV_A), BF16),
        compiler_params=_params("parallel", "arbitrary"),
        name="sample_attn",
    )(page_table.reshape(-1), scal, q, k_new, v_new, z, table2, subln_w.reshape(1, DV_A),
      *([cache_k] * pages), *([cache_v] * pages))


def _sample_ssd_body(x_ref, us_ref, z_ref, h0_ref, dtb_ref, alog_ref, dskip_ref, nw_ref, o_ref, h_ref, *, hd, n_heads):
    heads_per_group = n_heads // G_B
    dt = _softplus(us_ref[:, 0:n_heads] + dtb_ref[...])
    e_a = jnp.exp(dt * (-jnp.exp(alog_ref[...])))
    lane = lax.broadcasted_iota(jnp.int32, (1, LANES), 1)
    lo = lane < P_B
    row_lo = lax.broadcasted_iota(jnp.int32, (2 * P_B, N_B), 0) < P_B
    first = lax.broadcasted_iota(jnp.int32, (8, LANES), 0) == 0
    dskip = dskip_ref[...]
    ys = []
    for j in range(n_heads // 2):
        ha, hb = 2 * j, 2 * j + 1
        g = ha // heads_per_group
        bg = x_ref[:, hd + g * N_B:hd + (g + 1) * N_B]
        cg = x_ref[:, hd + (G_B + g) * N_B:hd + (G_B + g + 1) * N_B]
        cg_r = cg.astype(BF16).astype(F32)
        cb = jnp.sum(cg_r * bg.astype(BF16).astype(F32), axis=-1, keepdims=True)
        x = x_ref[:, j * LANES:(j + 1) * LANES]
        xdt = x * jnp.where(lo, dt[:, ha:ha + 1], dt[:, hb:hb + 1])
        h_prev = h0_ref[ha:hb + 1].reshape(2 * P_B, N_B)
        c8 = jnp.where(first, jnp.broadcast_to(cg, (8, N_B)), 0.0).astype(BF16)
        y_off = _dot_nt(c8, h_prev.astype(BF16))[0:1] * jnp.where(lo, e_a[:, ha:ha + 1], e_a[:, hb:hb + 1])
        x8 = jnp.where(first, jnp.broadcast_to(xdt, (8, LANES)), 0.0)
        b8 = jnp.where(first, jnp.broadcast_to(bg, (8, N_B)), 0.0)
        st = _dot_tn(x8, b8, HIGHEST)
        h_new = h_prev * jnp.where(row_lo, e_a[:, ha:ha + 1], e_a[:, hb:hb + 1]) + st
        h_ref[ha:hb + 1] = h_new.reshape(2, P_B, N_B)
        ys.append(cb * xdt + y_off + x * jnp.where(lo, dskip[:, ha:ha + 1], dskip[:, hb:hb + 1]))
    y = jnp.concatenate(ys, axis=1) * _silu(z_ref[...])
    r = lax.rsqrt(jnp.mean(y * y, axis=-1, keepdims=True) + EPS)
    o_ref[...] = ((y * r) * nw_ref[...]).astype(o_ref.dtype)


def sample_ssd(xc, us, z, h0, dt_bias, a_log, d_skip, norm_w):
    batch, n_heads = h0.shape[0], h0.shape[1]
    hd = n_heads * P_B
    cdim = xc.shape[2]
    rowspec = lambda w: pl.BlockSpec((None, 1, w), lambda b: (b, 0, 0))
    vec = pl.BlockSpec((1, n_heads), lambda b: (0, 0))
    st = pl.BlockSpec((None, n_heads, P_B, N_B), lambda b: (b, 0, 0, 0))
    return pl.pallas_call(
        functools.partial(_sample_ssd_body, hd=hd, n_heads=n_heads),
        grid=(batch,),
        in_specs=[rowspec(cdim), rowspec(LANES), rowspec(hd), st, vec, vec, vec, pl.BlockSpec((1, hd), lambda b: (0, 0))],
        out_specs=[rowspec(hd), st],
        out_shape=[jax.ShapeDtypeStruct((batch, 1, hd), BF16), jax.ShapeDtypeStruct(h0.shape, F32)],
        compiler_params=_params("parallel"),
        name="sample_ssd",
    )(xc, us, z, h0, dt_bias.reshape(1, -1), a_log.reshape(1, -1), d_skip.reshape(1, -1), norm_w.reshape(1, hd))


def _sample_gdn_body(us_ref, alog_ref, dtb_ref, q_ref, k_ref, v_ref, z_ref, s0_ref, nw_ref, o_ref, s_ref,
                     *, a_lane, b_lane):
    b = pl.program_id(0)
    h = pl.program_id(1)
    a_logit = jnp.full((1, LANES), us_ref[b, a_lane + h], F32)
    b_logit = jnp.full((1, LANES), us_ref[b, b_lane + h], F32)
    g = -jnp.exp(jnp.full((1, LANES), alog_ref[h], F32)) * _softplus(a_logit + dtb_ref[h])
    e_g = jnp.exp(g)
    beta = jax.nn.sigmoid(b_logit)
    q = _l2norm(q_ref[...]) * (DK_C ** -0.5)
    k = _l2norm(k_ref[...])
    v = v_ref[...]
    s0 = s0_ref[...]
    row = lax.broadcasted_iota(jnp.int32, (8, LANES), 0)
    lhs = jnp.where(row == 0, k * beta * e_g, jnp.where(row == 1, q * e_g, 0.0))
    ws = _dot(lhs, s0, HIGHEST)
    v_new = v * beta - ws[0:1]
    qk = jnp.sum(q * k, axis=-1, keepdims=True)
    o = ws[1:2] + qk * v_new
    k8 = jnp.where(row == 0, jnp.broadcast_to(k, (8, LANES)), 0.0)
    v8 = jnp.where(row == 0, jnp.broadcast_to(v_new, (8, LANES)), 0.0)
    s_ref[...] = s0 * e_g + _dot_tn(k8, v8, HIGHEST)
    r = lax.rsqrt(jnp.mean(o * o, axis=-1, keepdims=True) + EPS)
    o_ref[...] = ((o * r) * nw_ref[...] * _silu(z_ref[...])).astype(o_ref.dtype)


def sample_gdn(xc, us, z, s0, a_log, dt_bias, norm_w, a_lane, b_lane):
    batch, n_heads = s0.shape[0], s0.shape[1]
    smem = pl.BlockSpec(memory_space=pltpu.SMEM)
    tile = lambda off: pl.BlockSpec((None, None, 1, DK_C), lambda b, h: (b, off + h, 0, 0))
    st = pl.BlockSpec((None, None, DK_C, DK_C), lambda b, h: (b, h, 0, 0))
    return pl.pallas_call(
        functools.partial(_sample_gdn_body, a_lane=a_lane, b_lane=b_lane),
        grid=(batch, n_heads),
        in_specs=[smem, smem, smem, tile(0), tile(n_heads), tile(2 * n_heads), tile(0), st,
                  pl.BlockSpec((1, DK_C), lambda b, h: (0, 0))],
        out_specs=[tile(0), st],
        out_shape=[jax.ShapeDtypeStruct((batch, n_heads, 1, DK_C), BF16), jax.ShapeDtypeStruct(s0.shape, F32)],
        compiler_params=_params("parallel", "parallel"),
        name="sample_gdn",
    )(us, a_log, dt_bias, xc, xc, xc, z, s0, norm_w.reshape(1, DK_C))


def kernel(x_prompt, x_sample, cache_k, cache_v, page_table, state_ssm, state_conv_ssm, state_delta, state_conv_delta, rel_bias_table, norm_w, w_in, lam_q1, lam_k1, lam_q2, lam_k2, subln_w, conv_w_b, conv_bias_b, dt_bias_b, a_log_b, d_skip_b, norm_b_w, conv_w_c, dt_bias_c, a_log_c, norm_c_w, w_branch_a, w_branch_b, w_branch_c, w_out, final_norm_w):
    batch, seq, d = x_prompt.shape
    dec_batch = x_sample.shape[0]
    depth = w_in.shape[0]
    hd = d // 2
    h_a = hd // DV_A
    h_b = hd // P_B
    h_c = hd // DK_C
    conv_b = hd + 2 * G_B * N_B
    conv_c = 3 * hd
    m = batch * seq
    ms = 16

    sizes = (hd, hd, hd, hd, hd, conv_b, h_b, conv_c, hd, h_c, h_c, 3 * d)
    offs = np.concatenate([[0], np.cumsum(sizes)]).tolist()
    o_dt, o_qkvc, o_ac, o_bc, o_gate = offs[6], offs[7], offs[9], offs[10], offs[11]
    c_xbc = 5 * hd
    c_qkvc = c_xbc + conv_b
    c_zc = c_qkvc + conv_c
    c_gate = c_zc + hd
    n_u = c_gate + 3 * d
    bn_u = _pick(math.gcd(n_u, hd), (512, 256, 128))
    regions = ((0, c_qkvc // bn_u, 0), (c_qkvc // bn_u, c_gate // bn_u, o_qkvc - c_qkvc),
               (c_gate // bn_u, n_u // bn_u, o_gate - c_gate))
    blk_dt, blk_ab = o_dt // LANES, o_ac // LANES
    a_lane, b_lane = o_ac % LANES, o_bc % LANES
    assert o_dt % LANES == 0 and h_b <= LANES and b_lane + h_c <= LANES
    assert c_qkvc % bn_u == 0 and c_gate % bn_u == 0 and o_gate - c_gate < LANES

    w_t = jnp.swapaxes(w_in, 1, 2)

    t_attn = min(ATTN_BLOCK, seq)
    bias = bias_tiles(rel_bias_table, t_attn)
    table2 = jnp.tile(rel_bias_table.T, (2, 1))

    xp = x_prompt.reshape(m, d)
    xs = jnp.zeros((ms, d), F32).at[:dec_batch].set(x_sample.reshape(dec_batch, d))
    k_all = jnp.zeros((depth, m, hd), F32)
    v_all = jnp.zeros((depth, m, hd), F32)
    outs = {k: [] for k in ("ks", "vs", "ssmp", "ssms", "cbp", "cbs", "dp", "ds", "ccp", "ccs")}

    for l in range(depth):
        lam_init = 0.8 - 0.6 * math.exp(-0.3 * l)
        lam = jnp.exp(jnp.sum(lam_q1[l] * lam_k1[l])) - jnp.exp(jnp.sum(lam_q2[l] * lam_k2[l])) + lam_init
        scal = jnp.stack([lam, jnp.asarray(1.0 - lam_init, F32)]).astype(F32)
        zero_bias_c = jnp.zeros((conv_c,), F32)

        hp = rmsnorm(xp, norm_w[l], BF16)
        hs = rmsnorm(xs, norm_w[l], BF16)
        u, u_s, k_all, v_all = input_projection(hp, hs, w_t, l, hd, n_u, regions, k_all, v_all)
        us_dt, us_ab = small_logits(hp, w_t, l, blk_dt, blk_ab)
        u3 = u.reshape(batch, seq, -1)
        outs["cbp"].append(u3[:, seq - (CONV_K - 1):, c_xbc:c_xbc + conv_b].astype(F32))
        outs["ccp"].append(u3[:, seq - (CONV_K - 1):, c_qkvc:c_qkvc + conv_c].astype(F32))

        o_a = prompt_attention(u, bias, subln_w[l], scal, batch, seq, hd, t_attn)
        xc_b = prompt_conv(u, c_xbc, conv_b, conv_w_b[l], conv_bias_b[l], batch, seq)
        o_b, ssm = prompt_ssd(xc_b, us_dt, u, 4, dt_bias_b[l], a_log_b[l], d_skip_b[l], norm_b_w[l], batch, seq, hd)
        xc_c = prompt_conv(u, c_qkvc, conv_c, conv_w_c[l], zero_bias_c, batch, seq)
        o_c, delta = prompt_gdn(xc_c, us_ab, u, c_zc, a_log_c[l], dt_bias_c[l], norm_c_w[l],
                                batch, seq, hd, a_lane, b_lane)
        outs["ssmp"].append(ssm)
        outs["dp"].append(delta)
        merged = merge_branches(o_a, o_b, o_c, w_branch_a, w_branch_b, w_branch_c, l, u, c_gate)
        xp = out_projection(merged, w_out, l, xp)

        us_dt_s, us_ab_s = small_logits(hs, w_t, l, blk_dt, blk_ab)
        ur = u_s[:dec_batch]
        q_s = ur[:, 0:hd].reshape(dec_batch, h_a, DV_A)
        k_s = ur[:, hd:2 * hd].reshape(dec_batch, h_a, DV_A)
        v_s = ur[:, 2 * hd:3 * hd].reshape(dec_batch, h_a, DV_A)
        z_s = ur[:, 3 * hd:4 * hd].reshape(dec_batch, h_a, DV_A)
        outs["ks"].append(k_s.reshape(dec_batch, 1, h_a, DV_A))
        outs["vs"].append(v_s.reshape(dec_batch, 1, h_a, DV_A))
        oa_s = sample_attention(q_s, k_s, v_s, z_s, cache_k, cache_v, l, page_table, table2, subln_w[l], scal)

        xcb_s, cb_state = sample_conv(state_conv_ssm[l], ur[:, c_xbc:c_xbc + conv_b], conv_w_b[l], conv_bias_b[l])
        ob_s, ssm_s = sample_ssd(xcb_s.reshape(dec_batch, 1, conv_b), us_dt_s[:dec_batch].reshape(dec_batch, 1, LANES),
                                 ur[:, 4 * hd:5 * hd].reshape(dec_batch, 1, hd), state_ssm[l],
                                 dt_bias_b[l], a_log_b[l], d_skip_b[l], norm_b_w[l])
        xcc_s, cc_state = sample_conv(state_conv_delta[l], ur[:, c_qkvc:c_qkvc + conv_c], conv_w_c[l], zero_bias_c)
        oc_s, delta_s = sample_gdn(xcc_s.reshape(dec_batch, 3 * h_c, 1, DK_C), us_ab_s,
                                   ur[:, c_zc:c_zc + hd].reshape(dec_batch, h_c, 1, DK_C), state_delta[l],
                                   a_log_c[l], dt_bias_c[l], norm_c_w[l], a_lane, b_lane)
        outs["cbs"].append(cb_state)
        outs["ccs"].append(cc_state)
        outs["ssms"].append(ssm_s)
        outs["ds"].append(delta_s)

        def pad_rows(t):
            return jnp.zeros((ms, hd), BF16).at[:dec_batch].set(t.reshape(dec_batch, hd))

        merged_s = merge_branches(pad_rows(oa_s), pad_rows(ob_s), pad_rows(oc_s),
                                  w_branch_a, w_branch_b, w_branch_c, l, u_s, c_gate)
        xs = out_projection(merged_s, w_out, l, xs)

    y_prompt = rmsnorm(xp, final_norm_w, F32).reshape(batch, seq, d)
    y_sample = rmsnorm(xs, final_norm_w, F32)[:dec_batch].reshape(dec_batch, 1, d)
    st = {k: jnp.stack(v, axis=0) for k, v in outs.items()}
    k_prompt = k_all.reshape(depth, batch, seq, h_a, DV_A)
    v_prompt = v_all.reshape(depth, batch, seq, h_a, DV_A)
    return (y_prompt, y_sample, k_prompt, v_prompt, st["ks"], st["vs"], st["ssmp"], st["ssms"],
            st["cbp"], st["cbs"], st["dp"], st["ds"], st["ccp"], st["ccs"])
```

```python
import functools
import math

import numpy as np
import jax
import jax.numpy as jnp
from jax import lax
from jax.experimental import pallas as pl
from jax.experimental.pallas import tpu as pltpu

F32 = jnp.float32
BF16 = jnp.bfloat16
HIGHEST = lax.Precision.HIGHEST

DK_A = 64
DV_A = 128
NUM_BUCKETS = 32
MAX_DISTANCE = 128
PAGE_SIZE = 128
P_B = 64
G_B = 4
N_B = 128
DK_C = 128
CONV_K = 4
EPS = 1e-6
NEG = -1e30
LOG2E = math.log2(math.e)

LANES = 128
VMEM_LIMIT = 56 * 1024 * 1024

ATTN_BLOCK = 512
SSD_CHUNK = 128
GDN_CHUNK = 64
GDN_SUPER = 256
GDN_HEADS = 4
DECODE_PAGES = 8


def _params(*sem):
    return pltpu.CompilerParams(dimension_semantics=sem, vmem_limit_bytes=VMEM_LIMIT)


def _pick(n, cands):
    for c in cands:
        if n % c == 0:
            return c
    return n


def _silu(x):
    return x * jax.nn.sigmoid(x)


def _softplus(x):
    return jnp.maximum(x, 0.0) + jnp.log1p(jnp.exp(-jnp.abs(x)))


def _dot(a, b, precision=None):
    return jnp.dot(a, b, preferred_element_type=F32, precision=precision)


def _dot_nt(a, b, precision=None):
    return lax.dot_general(a, b, (((1,), (1,)), ((), ())), preferred_element_type=F32, precision=precision)


def _dot_tn(a, b, precision=None):
    return lax.dot_general(a, b, (((0,), (0,)), ((), ())), preferred_element_type=F32, precision=precision)


def _rmsnorm_body(x_ref, w_ref, o_ref):
    x = x_ref[...]
    r = lax.rsqrt(jnp.mean(x * x, axis=-1, keepdims=True) + EPS)
    o_ref[...] = ((x * r) * w_ref[...]).astype(o_ref.dtype)


def rmsnorm(x, w, out_dtype):
    m, d = x.shape
    bm = _pick(m, (256, 128, 64, 32, 16, 8))
    return pl.pallas_call(
        _rmsnorm_body,
        grid=(m // bm,),
        in_specs=[pl.BlockSpec((bm, d), lambda i: (i, 0)), pl.BlockSpec((1, d), lambda i: (0, 0))],
        out_specs=pl.BlockSpec((bm, d), lambda i: (i, 0)),
        out_shape=jax.ShapeDtypeStruct((m, d), out_dtype),
        compiler_params=_params("parallel"),
        name="rmsnorm",
    )(x, w.reshape(1, d))


def _stage_weights(w_ref, e_ref, dst_ref, shift):
    n = dst_ref.shape[0]
    if shift == 0:
        dst_ref[...] = w_ref[...].astype(BF16)
    else:
        dst_ref[0:n - shift, :] = w_ref[shift:n, :].astype(BF16)
        dst_ref[n - shift:n, :] = e_ref[0:shift, :].astype(BF16)


def _inproj_body(x_ref, xs_ref, w_ref, e_ref, k_in, v_in, o_ref, os_ref, k_ref, v_ref, wb_ref, *, tiles, regions):
    del k_in, v_in
    j = pl.program_id(0)
    i = pl.program_id(1)

    @pl.when(i == 0)
    def _():
        for lo, hi, shift in regions:
            @pl.when((j >= lo) & (j < hi))
            def _():
                _stage_weights(w_ref, e_ref, wb_ref, shift)
        os_ref[...] = _dot_nt(xs_ref[...], wb_ref[...])

    acc = _dot_nt(x_ref[...], wb_ref[...])
    o_ref[...] = acc.astype(o_ref.dtype)

    @pl.when((j >= tiles) & (j < 2 * tiles))
    def _():
        k_ref[...] = acc

    @pl.when((j >= 2 * tiles) & (j < 3 * tiles))
    def _():
        v_ref[...] = acc


def input_projection(x, xs, w_t, layer, hd, n_out, regions, k_all, v_all):
    m, k = x.shape
    ms = xs.shape[0]
    bm = _pick(m, (1024, 512, 256, 128, 64, 32, 16))
    bn = _pick(math.gcd(n_out, hd), (512, 256, 128))
    tiles = hd // bn
    nj, ni = n_out // bn, m // bm
    any_spec = pl.BlockSpec(memory_space=pl.ANY)

    def slab(first):
        def index(j, i):
            row = jnp.where(j < first, 0, jnp.where(j >= first + tiles, ni - 1, i))
            return (layer, row, jnp.clip(j - first, 0, tiles - 1))
        return pl.BlockSpec((None, bm, bn), index)

    return pl.pallas_call(
        functools.partial(_inproj_body, tiles=tiles, regions=regions),
        grid=(nj, ni),
        in_specs=[pl.BlockSpec((bm, k), lambda j, i: (i, 0)),
                  pl.BlockSpec((ms, k), lambda j, i: (0, 0)),
                  pl.BlockSpec((None, bn, k), lambda j, i: (layer, j, 0)),
                  pl.BlockSpec((None, LANES, k), lambda j, i: (layer, (j + 1) * (bn // LANES), 0)),
                  any_spec, any_spec],
        out_specs=[pl.BlockSpec((bm, bn), lambda j, i: (i, j)),
                   pl.BlockSpec((ms, bn), lambda j, i: (0, j)),
                   slab(tiles), slab(2 * tiles)],
        out_shape=[jax.ShapeDtypeStruct((m, n_out), BF16), jax.ShapeDtypeStruct((ms, n_out), F32),
                   jax.ShapeDtypeStruct(k_all.shape, F32), jax.ShapeDtypeStruct(v_all.shape, F32)],
        scratch_shapes=[pltpu.VMEM((bn, k), BF16)],
        input_output_aliases={4: 2, 5: 3},
        compiler_params=_params("arbitrary", "arbitrary"),
        name="inproj",
    )(x, xs, w_t, w_t, k_all, v_all)


def _small_logits_body(x_ref, wa_ref, wb_ref, oa_ref, ob_ref):
    x = x_ref[...]
    oa_ref[...] = _dot_nt(x, wa_ref[...].astype(BF16))
    ob_ref[...] = _dot_nt(x, wb_ref[...].astype(BF16))


def small_logits(x, w_t, layer, blk_a, blk_b):
    m, k = x.shape
    bm = _pick(m, (1024, 512, 256, 128, 64, 32, 16))
    wspec = lambda blk: pl.BlockSpec((None, LANES, k), lambda i: (layer, blk, 0))
    ospec = pl.BlockSpec((bm, LANES), lambda i: (i, 0))
    return pl.pallas_call(
        _small_logits_body,
        grid=(m // bm,),
        in_specs=[pl.BlockSpec((bm, k), lambda i: (i, 0)), wspec(blk_a), wspec(blk_b)],
        out_specs=[ospec, ospec],
        out_shape=[jax.ShapeDtypeStruct((m, LANES), F32)] * 2,
        compiler_params=_params("parallel"),
        name="small_logits",
    )(x, w_t, w_t)


def _merge_body(oa_ref, ob_ref, oc_ref, wa_ref, wb_ref, wc_ref, ga_ref, gb_ref, gc_ref, o_ref, sa, sb, sc):
    @pl.when(pl.program_id(1) == 0)
    def _():
        sa[...] = wa_ref[...].astype(BF16)
        sb[...] = wb_ref[...].astype(BF16)
        sc[...] = wc_ref[...].astype(BF16)

    acc = jax.nn.sigmoid(ga_ref[...].astype(F32)) * _dot(oa_ref[...], sa[...])
    acc = acc + jax.nn.sigmoid(gb_ref[...].astype(F32)) * _dot(ob_ref[...], sb[...])
    acc = acc + jax.nn.sigmoid(gc_ref[...].astype(F32)) * _dot(oc_ref[...], sc[...])
    o_ref[...] = acc.astype(o_ref.dtype)


def merge_branches(o_a, o_b, o_c, w_a, w_b, w_c, layer, u, gate_off):
    m, hd = o_a.shape
    d = w_a.shape[2]
    bm = _pick(m, (512, 256, 128, 64, 32, 16))
    bn = _pick(math.gcd(d, gate_off), (512, 256, 128))
    g0 = gate_off // bn
    nd = d // bn
    row = pl.BlockSpec((bm, hd), lambda j, i: (i, 0))
    wsp = pl.BlockSpec((None, hd, bn), lambda j, i: (layer, 0, j))

    def gate(t):
        return pl.BlockSpec((bm, bn), lambda j, i: (i, g0 + t * nd + j))

    return pl.pallas_call(
        _merge_body,
        grid=(nd, m // bm),
        in_specs=[row, row, row, wsp, wsp, wsp, gate(0), gate(1), gate(2)],
        out_specs=pl.BlockSpec((bm, bn), lambda j, i: (i, j)),
        out_shape=jax.ShapeDtypeStruct((m, d), BF16),
        scratch_shapes=[pltpu.VMEM((hd, bn), BF16)] * 3,
        compiler_params=_params("parallel", "arbitrary"),
        name="merge",
    )(o_a, o_b, o_c, w_a, w_b, w_c, u, u, u)


def _outproj_body(m_ref, w_ref, x_ref, o_ref, ws):
    @pl.when(pl.program_id(1) == 0)
    def _():
        ws[...] = w_ref[...].astype(BF16)

    o_ref[...] = x_ref[...] + _dot(m_ref[...], ws[...])


def out_projection(merged, w_out, layer, x):
    m, d = merged.shape
    bm = _pick(m, (1024, 512, 256, 128, 64, 32, 16))
    bn = _pick(d, (512, 256, 128))
    return pl.pallas_call(
        _outproj_body,
        grid=(d // bn, m // bm),
        in_specs=[pl.BlockSpec((bm, d), lambda j, i: (i, 0)), pl.BlockSpec((None, d, bn), lambda j, i: (layer, 0, j)),
                  pl.BlockSpec((bm, bn), lambda j, i: (i, j))],
        out_specs=pl.BlockSpec((bm, bn), lambda j, i: (i, j)),
        out_shape=jax.ShapeDtypeStruct((m, d), F32),
        scratch_shapes=[pltpu.VMEM((d, bn), BF16)],
        compiler_params=_params("parallel", "arbitrary"),
        name="outproj",
    )(merged, w_out, x)


def _bucket_changes():
    max_exact = NUM_BUCKETS // 2
    n = np.arange(0, MAX_DISTANCE + 1)
    nf = np.maximum(n, 1).astype(np.float32)
    large = max_exact + (np.log(nf / np.float32(max_exact)) / np.float32(math.log(MAX_DISTANCE / max_exact))
                         * np.float32(NUM_BUCKETS - max_exact)).astype(np.int32)
    bucket = np.where(n < max_exact, n, np.minimum(large, NUM_BUCKETS - 1))
    bucket[MAX_DISTANCE] = NUM_BUCKETS - 1
    changes = [(0, int(bucket[0]))]
    for d in range(1, MAX_DISTANCE + 1):
        if bucket[d] != bucket[d - 1]:
            changes.append((d, int(bucket[d])))
    return changes


def _bias_tiles_body(tbl_ref, o_ref, *, t, n_heads):
    h = pl.program_id(0)
    i = lax.broadcasted_iota(jnp.int32, (t, t), 0)
    j = lax.broadcasted_iota(jnp.int32, (t, t), 1)
    last = tbl_ref[(NUM_BUCKETS - 1) * n_heads + h]
    changes = _bucket_changes()
    for tile, off in ((0, 0), (1, t)):
        d = i - j + off
        val = jnp.full((t, t), (tbl_ref[changes[0][1] * n_heads + h] - last) * LOG2E, F32)
        for ds, b in changes[1:]:
            val = jnp.where(d >= ds, (tbl_ref[b * n_heads + h] - last) * LOG2E, val)
        if off == 0:
            val = jnp.where(j <= i, val, NEG)
        o_ref[tile] = val


def bias_tiles(table, t):
    n_heads = table.shape[1]
    return pl.pallas_call(
        functools.partial(_bias_tiles_body, t=t, n_heads=n_heads),
        grid=(n_heads,),
        in_specs=[pl.BlockSpec(memory_space=pltpu.SMEM)],
        out_specs=pl.BlockSpec((None, 2, t, t), lambda h: (h, 0, 0, 0)),
        out_shape=jax.ShapeDtypeStruct((n_heads, 2, t, t), F32),
        compiler_params=_params("parallel"),
        name="bias_tiles",
    )(table.reshape(-1))


def _attn_body(scal_ref, q_ref, kb_ref, vb_ref, z_ref, bias_ref, sw_ref, o_ref, *, t):
    qi = pl.program_id(2)
    q = q_ref[...].astype(F32) * (DK_A ** -0.5 * LOG2E)
    lane = lax.broadcasted_iota(jnp.int32, q.shape, 1)
    qq = jnp.concatenate([jnp.where(lane < DK_A, q, 0.0), jnp.where(lane >= DK_A, q, 0.0)], axis=0).astype(BF16)

    def step(kc, carry, bias_idx):
        m, l, acc = carry
        start = pl.multiple_of(kc * t, t)
        s = _dot_nt(qq, kb_ref[pl.ds(start, t), :])
        if bias_idx is not None:
            s = (s.reshape(2, t, t) + bias_ref[bias_idx][None]).reshape(2 * t, t)
        m_new = jnp.maximum(m, jnp.max(s, axis=-1, keepdims=True))
        alpha = jnp.exp2(m - m_new)
        p = jnp.exp2(s - m_new)
        l = alpha * l + jnp.sum(p, axis=-1, keepdims=True)
        acc = alpha * acc + _dot(p.astype(BF16), vb_ref[pl.ds(start, t), :])
        return m_new, l, acc

    carry = (jnp.full((2 * t, 1), NEG, F32), jnp.zeros((2 * t, 1), F32), jnp.zeros((2 * t, DV_A), F32))
    far = jnp.maximum(qi - 1, 0)
    carry = lax.fori_loop(0, far, lambda kc, c: step(kc, c, None), carry)
    carry = lax.fori_loop(far, qi, lambda kc, c: step(kc, c, 1), carry)
    _, l, acc = step(qi, carry, 0)

    o = acc / l
    o = o[:t] - scal_ref[0] * o[t:]
    r = lax.rsqrt(jnp.mean(o * o, axis=-1, keepdims=True) + EPS)
    o = (o * r) * sw_ref[...] * scal_ref[1]
    o_ref[...] = (o * _silu(z_ref[...].astype(F32))).astype(o_ref.dtype)


def prompt_attention(u, bias, subln_w, scal, batch, seq, hd, t):
    n_heads = hd // DV_A
    nq = seq // t
    return pl.pallas_call(
        functools.partial(_attn_body, t=t),
        grid=(batch, n_heads, nq),
        in_specs=[
            pl.BlockSpec(memory_space=pltpu.SMEM),
            pl.BlockSpec((t, DV_A), lambda b, h, qi: (b * nq + qi, h)),
            pl.BlockSpec((seq, DV_A), lambda b, h, qi: (b, n_heads + h)),
            pl.BlockSpec((seq, DV_A), lambda b, h, qi: (b, 2 * n_heads + h)),
            pl.BlockSpec((t, DV_A), lambda b, h, qi: (b * nq + qi, 3 * n_heads + h)),
            pl.BlockSpec((None, 2, t, t), lambda b, h, qi: (h, 0, 0, 0)),
            pl.BlockSpec((1, DV_A), lambda b, h, qi: (0, 0)),
        ],
        out_specs=pl.BlockSpec((t, DV_A), lambda b, h, qi: (b * nq + qi, h)),
        out_shape=jax.ShapeDtypeStruct((batch * seq, hd), BF16),
        compiler_params=_params("parallel", "parallel", "arbitrary"),
        name="prompt_attn",
    )(scal, u, u, u, u, bias, subln_w.reshape(1, DV_A))


def _conv_body(x_ref, w_ref, b_ref, o_ref):
    x = x_ref[...].astype(F32)
    row = lax.broadcasted_iota(jnp.int32, x.shape, 0)
    acc = x * w_ref[CONV_K - 1:CONV_K, :]
    for k in range(1, CONV_K):
        shifted = jnp.where(row >= k, pltpu.roll(x, k, axis=0), 0.0)
        acc = acc + shifted * w_ref[CONV_K - 1 - k:CONV_K - k, :]
    acc = acc + b_ref[...]
    o_ref[...] = _silu(acc)


def prompt_conv(u, col_off, width, w, bias, batch, seq):
    cw = _pick(math.gcd(col_off, width), (512, 256, 128))
    c0 = col_off // cw
    return pl.pallas_call(
        _conv_body,
        grid=(batch, width // cw),
        in_specs=[pl.BlockSpec((seq, cw), lambda b, c: (b, c0 + c)),
                  pl.BlockSpec((CONV_K, cw), lambda b, c: (0, c)),
                  pl.BlockSpec((1, cw), lambda b, c: (0, c))],
        out_specs=pl.BlockSpec((seq, cw), lambda b, c: (b, c)),
        out_shape=jax.ShapeDtypeStruct((batch * seq, width), F32),
        compiler_params=_params("parallel", "parallel"),
        name="prompt_conv",
    )(u, w, bias.reshape(1, width))


def _sample_conv_body(prev_ref, x_ref, w_ref, b_ref, y_ref, st_ref):
    x = x_ref[...]
    acc = x * w_ref[CONV_K - 1:CONV_K, :]
    for i in range(CONV_K - 1):
        acc = acc + prev_ref[i] * w_ref[i:i + 1, :]
    y_ref[...] = _silu(acc + b_ref[...])
    for i in range(CONV_K - 2):
        st_ref[i] = prev_ref[i + 1]
    st_ref[CONV_K - 2] = x


def sample_conv(prev, x, w, bias):
    rows, c = x.shape
    prev_t = jnp.transpose(prev, (1, 0, 2))
    cw = _pick(c, (1024, 512, 256, 128))
    y, st = pl.pallas_call(
        _sample_conv_body,
        grid=(c // cw,),
        in_specs=[pl.BlockSpec((CONV_K - 1, rows, cw), lambda j: (0, 0, j)),
                  pl.BlockSpec((rows, cw), lambda j: (0, j)),
                  pl.BlockSpec((CONV_K, cw), lambda j: (0, j)),
                  pl.BlockSpec((1, cw), lambda j: (0, j))],
        out_specs=[pl.BlockSpec((rows, cw), lambda j: (0, j)),
                   pl.BlockSpec((CONV_K - 1, rows, cw), lambda j: (0, 0, j))],
        out_shape=[jax.ShapeDtypeStruct((rows, c), F32), jax.ShapeDtypeStruct((CONV_K - 1, rows, c), F32)],
        compiler_params=_params("parallel"),
        name="sample_conv",
    )(prev_t, x, w, bias.reshape(1, c))
    return y, jnp.transpose(st, (1, 0, 2))


def _ssd_body(x_ref, us_ref, z_ref, dtb_ref, alog_ref, dskip_ref, nw_ref, o_ref, h_ref, *, t, hd, n_heads):
    c = pl.program_id(1)

    @pl.when(c == 0)
    def _():
        h_ref[...] = jnp.zeros_like(h_ref)

    heads_per_group = n_heads // G_B
    dt = _softplus(us_ref[:, 0:n_heads] + dtb_ref[...])
    a = dt * (-jnp.exp(alog_ref[...]))
    ri = lax.broadcasted_iota(jnp.int32, (t, t), 0)
    ci = lax.broadcasted_iota(jnp.int32, (t, t), 1)
    lower = ci <= ri
    a_col = _dot(lower.astype(F32), a, HIGHEST)
    a_row = _dot_tn(a, (ri <= ci).astype(F32), HIGHEST)
    a_last = a_col[t - 1:t, :]
    e_col = jnp.exp(a_col)
    e_end = jnp.exp(a_last - a_col)
    e_last = jnp.exp(a_last)
    lane = lax.broadcasted_iota(jnp.int32, (t, LANES), 1)
    lo = lane < P_B
    row_lo = lax.broadcasted_iota(jnp.int32, (2 * P_B, N_B), 0) < P_B
    dskip = dskip_ref[...]

    cb = {}
    ys = []
    for j in range(n_heads // 2):
        ha, hb = 2 * j, 2 * j + 1
        g = ha // heads_per_group
        bg = x_ref[:, hd + g * N_B:hd + (g + 1) * N_B].astype(BF16)
        cg = x_ref[:, hd + (G_B + g) * N_B:hd + (G_B + g + 1) * N_B].astype(BF16)
        if g not in cb:
            cb[g] = _dot_nt(cg, bg)
        x = x_ref[:, j * LANES:(j + 1) * LANES]
        xdt = x * jnp.where(lo, dt[:, ha:ha + 1], dt[:, hb:hb + 1])
        xdt_b = xdt.astype(BF16)
        y = None
        for hh, keep in ((ha, lo), (hb, ~lo)):
            seg = a_col[:, hh:hh + 1] - a_row[hh:hh + 1, :]
            w = (cb[g] * jnp.exp(jnp.where(lower, seg, NEG))).astype(BF16)
            part = _dot(w, jnp.where(keep, xdt_b, jnp.zeros_like(xdt_b)))
            y = part if y is None else y + part
        h_prev = h_ref[ha:hb + 1].reshape(2 * P_B, N_B)
        y = y + _dot_nt(cg, h_prev.astype(BF16)) * jnp.where(lo, e_col[:, ha:ha + 1], e_col[:, hb:hb + 1])
        xdec = (xdt * jnp.where(lo, e_end[:, ha:ha + 1], e_end[:, hb:hb + 1])).astype(BF16)
        st = _dot_tn(xdec, bg)
        h_new = h_prev * jnp.where(row_lo, e_last[:, ha:ha + 1], e_last[:, hb:hb + 1]) + st
        h_ref[ha:hb + 1] = h_new.reshape(2, P_B, N_B)
        y = y + x * jnp.where(lo[:1], dskip[:, ha:ha + 1], dskip[:, hb:hb + 1])
        ys.append(y)
    y = jnp.concatenate(ys, axis=1)
    y = y * _silu(z_ref[...].astype(F32))
    r = lax.rsqrt(jnp.mean(y * y, axis=-1, keepdims=True) + EPS)
    o_ref[...] = ((y * r) * nw_ref[...]).astype(o_ref.dtype)


def prompt_ssd(xc, us, u, z_blk, dt_bias, a_log, d_skip, norm_w, batch, seq, hd):
    n_heads = hd // P_B
    t = min(SSD_CHUNK, seq)
    nc = seq // t
    cdim = xc.shape[1]
    vec = pl.BlockSpec((1, n_heads), lambda b, c: (0, 0))
    return pl.pallas_call(
        functools.partial(_ssd_body, t=t, hd=hd, n_heads=n_heads),
        grid=(batch, nc),
        in_specs=[pl.BlockSpec((t, cdim), lambda b, c: (b * nc + c, 0)),
                  pl.BlockSpec((t, LANES), lambda b, c: (b * nc + c, 0)),
                  pl.BlockSpec((t, hd), lambda b, c: (b * nc + c, z_blk)),
                  vec, vec, vec,
                  pl.BlockSpec((1, hd), lambda b, c: (0, 0))],
        out_specs=[pl.BlockSpec((t, hd), lambda b, c: (b * nc + c, 0)),
                   pl.BlockSpec((None, n_heads, P_B, N_B), lambda b, c: (b, 0, 0, 0))],
        out_shape=[jax.ShapeDtypeStruct((batch * seq, hd), BF16),
                   jax.ShapeDtypeStruct((batch, n_heads, P_B, N_B), F32)],
        compiler_params=_params("parallel", "arbitrary"),
        name="prompt_ssd",
    )(xc, us, u, dt_bias.reshape(1, -1), a_log.reshape(1, -1), d_skip.reshape(1, -1), norm_w.reshape(1, hd))


def _l2norm(x):
    return x * lax.rsqrt(jnp.sum(x * x, axis=-1, keepdims=True) + EPS)


def _split3(x):
    a = x.astype(BF16).astype(F32)
    r = x - a
    b = r.astype(BF16).astype(F32)
    return a, b, (r - b).astype(BF16).astype(F32)


def _gdn_body(alog_ref, dtb_ref, q_ref, k_ref, v_ref, z_ref, us_ref, nw_ref, o_ref, s_ref,
              u_s, wq_s, qk_s, kd_s, gl_s, *, tc, sb, nc, a_lane, b_lane):
    pid = pl.program_id(1)
    cps = sb // tc
    shift = tc.bit_length() - 1
    ri = lax.broadcasted_iota(jnp.int32, (sb, sb), 0)
    ci = lax.broadcasted_iota(jnp.int32, (sb, sb), 1)
    same = lax.shift_right_logical(ri, shift) == lax.shift_right_logical(ci, shift)
    incl = same & (ci <= ri)
    strict = same & (ci < ri)
    eye = jnp.where(ri == ci, 1.0, 0.0)
    tri = jnp.where(incl, 1.0, 0.0).astype(BF16)
    tri_t = jnp.where(same & (ri <= ci), 1.0, 0.0).astype(BF16)
    tri_ones = jnp.concatenate([tri, jnp.where(same, 1.0, 0.0).astype(BF16)], axis=0)
    lane = lax.broadcasted_iota(jnp.int32, (sb, LANES), 1)

    def sum3(x):
        return x[:, 0:1] + x[:, 1:2] + x[:, 2:3]

    heads = range(GDN_HEADS)

    def each(f, *lists):
        return [f(*args) for args in zip(*lists)]

    def prepare_all(sidx, carry):
        rows = pl.ds(pl.multiple_of(sidx * sb, sb), sb)
        us = us_ref[rows, :]
        hs = [pid * GDN_HEADS + hh for hh in heads]
        cols = [slice(hh * DK_C, (hh + 1) * DK_C) for hh in heads]
        a_col = each(lambda h: jnp.sum(jnp.where(lane == a_lane + h, us, 0.0), axis=-1, keepdims=True), hs)
        b_col = each(lambda h: jnp.sum(jnp.where(lane == b_lane + h, us, 0.0), axis=-1, keepdims=True), hs)
        g = each(lambda h, a: -jnp.exp(jnp.full((1, 1), alog_ref[h], F32)) * _softplus(a + dtb_ref[h]), hs, a_col)
        beta = each(jax.nn.sigmoid, b_col)
        parts = each(_split3, g)
        gm = each(lambda t: jnp.where(lane == 0, t[0], jnp.where(lane == 1, t[1], jnp.where(lane == 2, t[2], 0.0)))
                  .astype(BF16), parts)
        cum = each(lambda x: _dot(tri_ones, x), gm)
        cum_t = each(lambda x: _dot_tn(x, tri_t), gm)
        gc = each(lambda x: sum3(x[:sb]), cum)
        g_end = each(lambda x: sum3(x[sb:]), cum)
        gc_row = each(lambda x: x[0:1] + x[1:2] + x[2:3], cum_t)
        decay = each(lambda c, r: jnp.exp(jnp.where(incl, c - r, NEG)), gc, gc_row)
        q = each(lambda c: _l2norm(q_ref[rows, c]) * (DK_C ** -0.5), cols)
        k = each(lambda c: _l2norm(k_ref[rows, c]), cols)
        kb = each(lambda x, b: x * b, k, beta)
        k_b = each(lambda x: x.astype(BF16), k)
        kq = each(lambda a, b, c: _dot_nt(jnp.concatenate([a, b], axis=0).astype(BF16), c), kb, q, k_b)
        m = each(lambda x, d: jnp.where(strict, x[:sb] * d, 0.0), kq, decay)
        x = each(lambda a: eye - a, m)
        p = each(lambda a: a.astype(BF16), m)
        n = 2
        while n <= tc // 2:
            p = each(lambda a: _dot(a, a).astype(BF16), p)
            x = each(lambda a, b: a + _dot(a.astype(BF16), b), x, p)
            n *= 2
        a_mat = each(lambda a: eye + a, m)
        a_hi = each(lambda a: a.astype(BF16), a_mat)
        a_lo = each(lambda a, b: (a - b.astype(F32)).astype(BF16), a_mat, a_hi)
        x_hi = each(lambda a: a.astype(BF16), x)
        x_lo = each(lambda a, b: (a - b.astype(F32)).astype(BF16), x, x_hi)
        y = each(lambda a, b, c: _dot(jnp.concatenate([a, b], axis=0), c), a_hi, a_lo, x_hi)
        y2 = each(_dot, a_hi, x_lo)
        resid = each(lambda a, b: eye - (a[:sb] + a[sb:] + b), y, y2)
        t_inv = each(lambda a, b, c: (a + _dot(b, c.astype(BF16))).astype(BF16), x, x_hi, resid)
        e_gc = each(jnp.exp, gc)
        rhs = each(lambda c, b, a, e: jnp.concatenate([v_ref[rows, c] * b, a * e], axis=1).astype(BF16),
                   cols, beta, kb, e_gc)
        uw = each(_dot, t_inv, rhs)
        qk = each(lambda a, d: (a[sb:] * d).astype(BF16), kq, decay)
        for hh in heads:
            u_s[hh, rows, :] = uw[hh][:, :DK_C]
            w = uw[hh][:, DK_C:].astype(BF16)
            qd = (q[hh] * e_gc[hh]).astype(BF16)
            kd_s[hh, rows, :] = (k[hh] * jnp.exp(g_end[hh] - gc[hh])).astype(BF16)
            e_end = jnp.exp(g_end[hh])
            for c in range(cps):
                cidx = sidx * cps + c
                r0 = c * tc
                wq_s[hh, cidx, 0:tc, :] = w[r0:r0 + tc]
                wq_s[hh, cidx, tc:2 * tc, :] = qd[r0:r0 + tc]
                qk_s[hh, cidx] = qk[hh][r0:r0 + tc, r0:r0 + tc]
                gl_s[hh, pl.ds(cidx, 1), :] = jnp.broadcast_to(e_end[r0:r0 + 1, :], (1, LANES))
        return carry

    lax.fori_loop(0, nc // cps, prepare_all, 0)

    def recur(cidx, states):
        rows = pl.ds(pl.multiple_of(cidx * tc, tc), tc)
        new = []
        for hh in range(GDN_HEADS):
            cols = slice(hh * DK_C, (hh + 1) * DK_C)
            s = states[hh]
            s_b = s.astype(BF16)
            ws = _dot(wq_s[hh, cidx], s_b)
            v_b = (u_s[hh, rows, :] - ws[:tc]).astype(BF16)
            o = ws[tc:] + _dot(qk_s[hh, cidx], v_b)
            r = lax.rsqrt(jnp.mean(o * o, axis=-1, keepdims=True) + EPS)
            o_ref[rows, cols] = ((o * r) * nw_ref[...] * _silu(z_ref[rows, cols].astype(F32))).astype(o_ref.dtype)
            new.append(s * gl_s[hh, pl.ds(cidx, 1), :] + _dot_tn(kd_s[hh, rows, :], v_b))
        return tuple(new)

    final = lax.fori_loop(0, nc, recur, tuple(jnp.zeros((DK_C, DK_C), F32) for _ in range(GDN_HEADS)))
    for hh in range(GDN_HEADS):
        s_ref[hh] = final[hh]


def prompt_gdn(xc, us, u, z_col, a_log, dt_bias, norm_w, batch, seq, hd, a_lane, b_lane):
    n_heads = hd // DK_C
    tc = min(GDN_CHUNK, seq)
    sb = min(GDN_SUPER, seq)
    nc = seq // tc
    gw = GDN_HEADS * DK_C
    smem = pl.BlockSpec(memory_space=pltpu.SMEM)
    heads = lambda col: pl.BlockSpec((seq, gw), lambda b, h: (b, col // gw + h))
    return pl.pallas_call(
        functools.partial(_gdn_body, tc=tc, sb=sb, nc=nc, a_lane=a_lane, b_lane=b_lane),
        grid=(batch, n_heads // GDN_HEADS),
        in_specs=[smem, smem, heads(0), heads(hd), heads(2 * hd), heads(z_col),
                  pl.BlockSpec((seq, LANES), lambda b, h: (b, 0)),
                  pl.BlockSpec((1, DK_C), lambda b, h: (0, 0))],
        out_specs=[pl.BlockSpec((seq, gw), lambda b, h: (b, h)),
                   pl.BlockSpec((None, GDN_HEADS, DK_C, DK_C), lambda b, h: (b, h, 0, 0))],
        out_shape=[jax.ShapeDtypeStruct((batch * seq, hd), BF16),
                   jax.ShapeDtypeStruct((batch, n_heads, DK_C, DK_C), F32)],
        scratch_shapes=[pltpu.VMEM((GDN_HEADS, seq, DK_C), F32),
                        pltpu.VMEM((GDN_HEADS, nc, 2 * tc, DK_C), BF16),
                        pltpu.VMEM((GDN_HEADS, nc, tc, tc), BF16),
                        pltpu.VMEM((GDN_HEADS, seq, DK_C), BF16),
                        pltpu.VMEM((GDN_HEADS, nc, LANES), F32)],
        compiler_params=_params("parallel", "parallel"),
        name="prompt_gdn",
    )(a_log, dt_bias, xc, xc, xc, u, us, norm_w.reshape(1, DK_C))


def _sample_attn_body(pt_ref, scal_ref, q_ref, kn_ref, vn_ref, z_ref, tbl_ref, sw_ref, *refs,
                      n_heads, n_steps, pages):
    k_refs, v_refs = refs[:pages], refs[pages:2 * pages]
    o_ref, m_s, l_s, acc_s = refs[2 * pages:]
    p = pl.program_id(1)
    rows = 2 * n_heads
    cols = PAGE_SIZE * n_heads

    @pl.when(p == 0)
    def _():
        m_s[...] = jnp.full_like(m_s, NEG)
        l_s[...] = jnp.zeros_like(l_s)
        acc_s[...] = jnp.zeros_like(acc_s)

    q = q_ref[...] * (DK_A ** -0.5)
    lane = lax.broadcasted_iota(jnp.int32, q.shape, 1)
    qq = jnp.concatenate([jnp.where(lane < DK_A, q, 0.0), jnp.where(lane >= DK_A, q, 0.0)], axis=0).astype(BF16)
    r_i = lax.broadcasted_iota(jnp.int32, (rows, cols), 0)
    c_i = lax.broadcasted_iota(jnp.int32, (rows, cols), 1)
    own = (c_i % n_heads) == (r_i % n_heads)
    last = tbl_ref[:, NUM_BUCKETS - 1:NUM_BUCKETS]
    changes = _bucket_changes()
    delta0 = tbl_ref[:, changes[0][1]:changes[0][1] + 1] - last

    def add_bias(s):
        d = PAGE_SIZE - c_i // n_heads
        val = jnp.broadcast_to(delta0, (rows, cols))
        for ds, b in changes[1:]:
            val = jnp.where(d >= ds, tbl_ref[:, b:b + 1] - last, val)
        return s + val

    ss = [_dot_nt(qq, k_refs[i][...].reshape(cols, DV_A).astype(BF16)) for i in range(pages)]
    ss[-1] = lax.cond(p == n_steps - 1, add_bias, lambda s: s, ss[-1])
    ss = [jnp.where(own, s, NEG) for s in ss]
    m_new = m_s[...]
    for s in ss:
        m_new = jnp.maximum(m_new, jnp.max(s, axis=-1, keepdims=True))
    alpha = jnp.exp(m_s[...] - m_new)
    l_new = alpha * l_s[...]
    acc = alpha * acc_s[...]
    for i, s in enumerate(ss):
        pr = jnp.where(own, jnp.exp(s - m_new), 0.0)
        l_new = l_new + jnp.sum(pr, axis=-1, keepdims=True)
        acc = acc + _dot(pr.astype(BF16), v_refs[i][...].reshape(cols, DV_A).astype(BF16))
    m_s[...] = m_new
    l_s[...] = l_new
    acc_s[...] = acc

    @pl.when(p == n_steps - 1)
    def _():
        kn = jnp.concatenate([kn_ref[...], kn_ref[...]], axis=0).astype(BF16).astype(F32)
        vn = jnp.concatenate([vn_ref[...], vn_ref[...]], axis=0).astype(BF16).astype(F32)
        s_new = jnp.sum(qq.astype(F32) * kn, axis=-1, keepdims=True) + delta0
        m_fin = jnp.maximum(m_new, s_new)
        a2 = jnp.exp(m_new - m_fin)
        p_new = jnp.exp(s_new - m_fin)
        l = a2 * l_new + p_new
        o = (a2 * acc + p_new.astype(BF16).astype(F32) * vn) / l
        o = o[:n_heads] - scal_ref[0] * o[n_heads:]
        r = lax.rsqrt(jnp.mean(o * o, axis=-1, keepdims=True) + EPS)
        o = (o * r) * sw_ref[...] * scal_ref[1]
        o_ref[...] = (o * _silu(z_ref[...])).astype(o_ref.dtype)


def sample_attention(q, k_new, v_new, z, cache_k, cache_v, layer, page_table, table2, subln_w, scal):
    batch, n_heads, _ = q.shape
    n_pages = page_table.shape[1]
    pages = _pick(n_pages, (DECODE_PAGES, 2, 1))
    n_steps = n_pages // pages
    per_seq = pl.BlockSpec((None, n_heads, DV_A), lambda b, p, pt: (b, 0, 0))

    def page(i):
        return pl.BlockSpec((None, None, PAGE_SIZE, n_heads, DV_A),
                            lambda b, p, pt: (layer, pt[b * n_pages + p * pages + i], 0, 0, 0))

    grid_spec = pltpu.PrefetchScalarGridSpec(
        num_scalar_prefetch=1,
        grid=(batch, n_steps),
        in_specs=[pl.BlockSpec(memory_space=pltpu.SMEM), per_seq, per_seq, per_seq, per_seq,
                  pl.BlockSpec((2 * n_heads, NUM_BUCKETS), lambda b, p, pt: (0, 0)),
                  pl.BlockSpec((1, DV_A), lambda b, p, pt: (0, 0))]
                 + [page(i) for i in range(pages)] + [page(i) for i in range(pages)],
        out_specs=per_seq,
        scratch_shapes=[pltpu.VMEM((2 * n_heads, 1), F32), pltpu.VMEM((2 * n_heads, 1), F32),
                        pltpu.VMEM((2 * n_heads, DV_A), F32)],
    )
    return pl.pallas_call(
        functools.partial(_sample_attn_body, n_heads=n_heads, n_steps=n_steps, pages=pages),
        grid_spec=grid_spec,
        out_shape=jax.ShapeDtypeStruct((batch, n_heads, DV_A), BF16),
        compiler_params=_params("parallel", "arbitrary"),
        name="sample_attn",
    )(page_table.reshape(-1), scal, q, k_new, v_new, z, table2, subln_w.reshape(1, DV_A),
      *([cache_k] * pages), *([cache_v] * pages))


def _sample_ssd_body(x_ref, us_ref, z_ref, h0_ref, dtb_ref, alog_ref, dskip_ref, nw_ref, o_ref, h_ref, *, hd, n_heads):
    heads_per_group = n_heads // G_B
    dt = _softplus(us_ref[:, 0:n_heads] + dtb_ref[...])
    e_a = jnp.exp(dt * (-jnp.exp(alog_ref[...])))
    lane = lax.broadcasted_iota(jnp.int32, (1, LANES), 1)
    lo = lane < P_B
    row_lo = lax.broadcasted_iota(jnp.int32, (2 * P_B, N_B), 0) < P_B
    first = lax.broadcasted_iota(jnp.int32, (8, LANES), 0) == 0
    dskip = dskip_ref[...]
    ys = []
    for j in range(n_heads // 2):
        ha, hb = 2 * j, 2 * j + 1
        g = ha // heads_per_group
        bg = x_ref[:, hd + g * N_B:hd + (g + 1) * N_B]
        cg = x_ref[:, hd + (G_B + g) * N_B:hd + (G_B + g + 1) * N_B]
        cg_r = cg.astype(BF16).astype(F32)
        cb = jnp.sum(cg_r * bg.astype(BF16).astype(F32), axis=-1, keepdims=True)
        x = x_ref[:, j * LANES:(j + 1) * LANES]
        xdt = x * jnp.where(lo, dt[:, ha:ha + 1], dt[:, hb:hb + 1])
        h_prev = h0_ref[ha:hb + 1].reshape(2 * P_B, N_B)
        c8 = jnp.where(first, jnp.broadcast_to(cg, (8, N_B)), 0.0).astype(BF16)
        y_off = _dot_nt(c8, h_prev.astype(BF16))[0:1] * jnp.where(lo, e_a[:, ha:ha + 1], e_a[:, hb:hb + 1])
        x8 = jnp.where(first, jnp.broadcast_to(xdt, (8, LANES)), 0.0)
        b8 = jnp.where(first, jnp.broadcast_to(bg, (8, N_B)), 0.0)
        st = _dot_tn(x8, b8, HIGHEST)
        h_new = h_prev * jnp.where(row_lo, e_a[:, ha:ha + 1], e_a[:, hb:hb + 1]) + st
        h_ref[ha:hb + 1] = h_new.reshape(2, P_B, N_B)
        ys.append(cb * xdt + y_off + x * jnp.where(lo, dskip[:, ha:ha + 1], dskip[:, hb:hb + 1]))
    y = jnp.concatenate(ys, axis=1) * _silu(z_ref[...])
    r = lax.rsqrt(jnp.mean(y * y, axis=-1, keepdims=True) + EPS)
    o_ref[...] = ((y * r) * nw_ref[...]).astype(o_ref.dtype)


def sample_ssd(xc, us, z, h0, dt_bias, a_log, d_skip, norm_w):
    batch, n_heads = h0.shape[0], h0.shape[1]
    hd = n_heads * P_B
    cdim = xc.shape[2]
    rowspec = lambda w: pl.BlockSpec((None, 1, w), lambda b: (b, 0, 0))
    vec = pl.BlockSpec((1, n_heads), lambda b: (0, 0))
    st = pl.BlockSpec((None, n_heads, P_B, N_B), lambda b: (b, 0, 0, 0))
    return pl.pallas_call(
        functools.partial(_sample_ssd_body, hd=hd, n_heads=n_heads),
        grid=(batch,),
        in_specs=[rowspec(cdim), rowspec(LANES), rowspec(hd), st, vec, vec, vec, pl.BlockSpec((1, hd), lambda b: (0, 0))],
        out_specs=[rowspec(hd), st],
        out_shape=[jax.ShapeDtypeStruct((batch, 1, hd), BF16), jax.ShapeDtypeStruct(h0.shape, F32)],
        compiler_params=_params("parallel"),
        name="sample_ssd",
    )(xc, us, z, h0, dt_bias.reshape(1, -1), a_log.reshape(1, -1), d_skip.reshape(1, -1), norm_w.reshape(1, hd))


def _sample_gdn_body(us_ref, alog_ref, dtb_ref, q_ref, k_ref, v_ref, z_ref, s0_ref, nw_ref, o_ref, s_ref,
                     *, a_lane, b_lane):
    b = pl.program_id(0)
    h = pl.program_id(1)
    a_logit = jnp.full((1, LANES), us_ref[b, a_lane + h], F32)
    b_logit = jnp.full((1, LANES), us_ref[b, b_lane + h], F32)
    g = -jnp.exp(jnp.full((1, LANES), alog_ref[h], F32)) * _softplus(a_logit + dtb_ref[h])
    e_g = jnp.exp(g)
    beta = jax.nn.sigmoid(b_logit)
    q = _l2norm(q_ref[...]) * (DK_C ** -0.5)
    k = _l2norm(k_ref[...])
    v = v_ref[...]
    s0 = s0_ref[...]
    row = lax.broadcasted_iota(jnp.int32, (8, LANES), 0)
    lhs = jnp.where(row == 0, k * beta * e_g, jnp.where(row == 1, q * e_g, 0.0))
    ws = _dot(lhs, s0, HIGHEST)
    v_new = v * beta - ws[0:1]
    qk = jnp.sum(q * k, axis=-1, keepdims=True)
    o = ws[1:2] + qk * v_new
    k8 = jnp.where(row == 0, jnp.broadcast_to(k, (8, LANES)), 0.0)
    v8 = jnp.where(row == 0, jnp.broadcast_to(v_new, (8, LANES)), 0.0)
    s_ref[...] = s0 * e_g + _dot_tn(k8, v8, HIGHEST)
    r = lax.rsqrt(jnp.mean(o * o, axis=-1, keepdims=True) + EPS)
    o_ref[...] = ((o * r) * nw_ref[...] * _silu(z_ref[...])).astype(o_ref.dtype)


def sample_gdn(xc, us, z, s0, a_log, dt_bias, norm_w, a_lane, b_lane):
    batch, n_heads = s0.shape[0], s0.shape[1]
    smem = pl.BlockSpec(memory_space=pltpu.SMEM)
    tile = lambda off: pl.BlockSpec((None, None, 1, DK_C), lambda b, h: (b, off + h, 0, 0))
    st = pl.BlockSpec((None, None, DK_C, DK_C), lambda b, h: (b, h, 0, 0))
    return pl.pallas_call(
        functools.partial(_sample_gdn_body, a_lane=a_lane, b_lane=b_lane),
        grid=(batch, n_heads),
        in_specs=[smem, smem, smem, tile(0), tile(n_heads), tile(2 * n_heads), tile(0), st,
                  pl.BlockSpec((1, DK_C), lambda b, h: (0, 0))],
        out_specs=[tile(0), st],
        out_shape=[jax.ShapeDtypeStruct((batch, n_heads, 1, DK_C), BF16), jax.ShapeDtypeStruct(s0.shape, F32)],
        compiler_params=_params("parallel", "parallel"),
        name="sample_gdn",
    )(us, a_log, dt_bias, xc, xc, xc, z, s0, norm_w.reshape(1, DK_C))


def kernel(x_prompt, x_sample, cache_k, cache_v, page_table, state_ssm, state_conv_ssm, state_delta, state_conv_delta, rel_bias_table, norm_w, w_in, lam_q1, lam_k1, lam_q2, lam_k2, subln_w, conv_w_b, conv_bias_b, dt_bias_b, a_log_b, d_skip_b, norm_b_w, conv_w_c, dt_bias_c, a_log_c, norm_c_w, w_branch_a, w_branch_b, w_branch_c, w_out, final_norm_w):
    batch, seq, d = x_prompt.shape
    dec_batch = x_sample.shape[0]
    depth = w_in.shape[0]
    hd = d // 2
    h_a = hd // DV_A
    h_b = hd // P_B
    h_c = hd // DK_C
    conv_b = hd + 2 * G_B * N_B
    conv_c = 3 * hd
    m = batch * seq
    ms = 16

    sizes = (hd, hd, hd, hd, hd, conv_b, h_b, conv_c, hd, h_c, h_c, 3 * d)
    offs = np.concatenate([[0], np.cumsum(sizes)]).tolist()
    o_dt, o_qkvc, o_ac, o_bc, o_gate = offs[6], offs[7], offs[9], offs[10], offs[11]
    c_xbc = 5 * hd
    c_qkvc = c_xbc + conv_b
    c_zc = c_qkvc + conv_c
    c_gate = c_zc + hd
    n_u = c_gate + 3 * d
    bn_u = _pick(math.gcd(n_u, hd), (512, 256, 128))
    regions = ((0, c_qkvc // bn_u, 0), (c_qkvc // bn_u, c_gate // bn_u, o_qkvc - c_qkvc),
               (c_gate // bn_u, n_u // bn_u, o_gate - c_gate))
    blk_dt, blk_ab = o_dt // LANES, o_ac // LANES
    a_lane, b_lane = o_ac % LANES, o_bc % LANES
    assert o_dt % LANES == 0 and h_b <= LANES and b_lane + h_c <= LANES
    assert c_qkvc % bn_u == 0 and c_gate % bn_u == 0 and o_gate - c_gate < LANES

    w_t = jnp.swapaxes(w_in, 1, 2)

    t_attn = min(ATTN_BLOCK, seq)
    bias = bias_tiles(rel_bias_table, t_attn)
    table2 = jnp.tile(rel_bias_table.T, (2, 1))

    xp = x_prompt.reshape(m, d)
    xs = jnp.zeros((ms, d), F32).at[:dec_batch].set(x_sample.reshape(dec_batch, d))
    k_all = jnp.zeros((depth, m, hd), F32)
    v_all = jnp.zeros((depth, m, hd), F32)
    outs = {k: [] for k in ("ks", "vs", "ssmp", "ssms", "cbp", "cbs", "dp", "ds", "ccp", "ccs")}

    for l in range(depth):
        lam_init = 0.8 - 0.6 * math.exp(-0.3 * l)
        lam = jnp.exp(jnp.sum(lam_q1[l] * lam_k1[l])) - jnp.exp(jnp.sum(lam_q2[l] * lam_k2[l])) + lam_init
        scal = jnp.stack([lam, jnp.asarray(1.0 - lam_init, F32)]).astype(F32)
        zero_bias_c = jnp.zeros((conv_c,), F32)

        hp = rmsnorm(xp, norm_w[l], BF16)
        hs = rmsnorm(xs, norm_w[l], BF16)
        u, u_s, k_all, v_all = input_projection(hp, hs, w_t, l, hd, n_u, regions, k_all, v_all)
        us_dt, us_ab = small_logits(hp, w_t, l, blk_dt, blk_ab)
        u3 = u.reshape(batch, seq, -1)
        outs["cbp"].append(u3[:, seq - (CONV_K - 1):, c_xbc:c_xbc + conv_b].astype(F32))
        outs["ccp"].append(u3[:, seq - (CONV_K - 1):, c_qkvc:c_qkvc + conv_c].astype(F32))

        o_a = prompt_attention(u, bias, subln_w[l], scal, batch, seq, hd, t_attn)
        xc_b = prompt_conv(u, c_xbc, conv_b, conv_w_b[l], conv_bias_b[l], batch, seq)
        o_b, ssm = prompt_ssd(xc_b, us_dt, u, 4, dt_bias_b[l], a_log_b[l], d_skip_b[l], norm_b_w[l], batch, seq, hd)
        xc_c = prompt_conv(u, c_qkvc, conv_c, conv_w_c[l], zero_bias_c, batch, seq)
        o_c, delta = prompt_gdn(xc_c, us_ab, u, c_zc, a_log_c[l], dt_bias_c[l], norm_c_w[l],
                                batch, seq, hd, a_lane, b_lane)
        outs["ssmp"].append(ssm)
        outs["dp"].append(delta)
        merged = merge_branches(o_a, o_b, o_c, w_branch_a, w_branch_b, w_branch_c, l, u, c_gate)
        xp = out_projection(merged, w_out, l, xp)

        us_dt_s, us_ab_s = small_logits(hs, w_t, l, blk_dt, blk_ab)
        ur = u_s[:dec_batch]
        q_s = ur[:, 0:hd].reshape(dec_batch, h_a, DV_A)
        k_s = ur[:, hd:2 * hd].reshape(dec_batch, h_a, DV_A)
        v_s = ur[:, 2 * hd:3 * hd].reshape(dec_batch, h_a, DV_A)
        z_s = ur[:, 3 * hd:4 * hd].reshape(dec_batch, h_a, DV_A)
        outs["ks"].append(k_s.reshape(dec_batch, 1, h_a, DV_A))
        outs["vs"].append(v_s.reshape(dec_batch, 1, h_a, DV_A))
        oa_s = sample_attention(q_s, k_s, v_s, z_s, cache_k, cache_v, l, page_table, table2, subln_w[l], scal)

        xcb_s, cb_state = sample_conv(state_conv_ssm[l], ur[:, c_xbc:c_xbc + conv_b], conv_w_b[l], conv_bias_b[l])
        ob_s, ssm_s = sample_ssd(xcb_s.reshape(dec_batch, 1, conv_b), us_dt_s[:dec_batch].reshape(dec_batch, 1, LANES),
                                 ur[:, 4 * hd:5 * hd].reshape(dec_batch, 1, hd), state_ssm[l],
                                 dt_bias_b[l], a_log_b[l], d_skip_b[l], norm_b_w[l])
        xcc_s, cc_state = sample_conv(state_conv_delta[l], ur[:, c_qkvc:c_qkvc + conv_c], conv_w_c[l], zero_bias_c)
        oc_s, delta_s = sample_gdn(xcc_s.reshape(dec_batch, 3 * h_c, 1, DK_C), us_ab_s,
                                   ur[:, c_zc:c_zc + hd].reshape(dec_batch, h_c, 1, DK_C), state_delta[l],
                                   a_log_c[l], dt_bias_c[l], norm_c_w[l], a_lane, b_lane)
        outs["cbs"].append(cb_state)
        outs["ccs"].append(cc_state)
        outs["ssms"].append(ssm_s)
        outs["ds"].append(delta_s)

        def pad_rows(t):
            return jnp.zeros((ms, hd), BF16).at[:dec_batch].set(t.reshape(dec_batch, hd))

        merged_s = merge_branches(pad_rows(oa_s), pad_rows(ob_s), pad_rows(oc_s),
                                  w_branch_a, w_branch_b, w_branch_c, l, u_s, c_gate)
        xs = out_projection(merged_s, w_out, l, xs)

    y_prompt = rmsnorm(xp, final_norm_w, F32).reshape(batch, seq, d)
    y_sample = rmsnorm(xs, final_norm_w, F32)[:dec_batch].reshape(dec_batch, 1, d)
    st = {k: jnp.stack(v, axis=0) for k, v in outs.items()}
    k_prompt = k_all.reshape(depth, batch, seq, h_a, DV_A)
    v_prompt = v_all.reshape(depth, batch, seq, h_a, DV_A)
    return (y_prompt, y_sample, k_prompt, v_prompt, st["ks"], st["vs"], st["ssmp"], st["ssms"],
            st["cbp"], st["cbs"], st["dp"], st["ds"], st["ccp"], st["ccs"])
```
